```python
import jax, jax.numpy as jnp
from jax import lax
import numpy as np

D_MODEL = 1024
BATCH = 8
SEQ = 2048
DEPTH = 1
DEC_BATCH = 128
DEC_SEQ = 4
PAST_LEN = 16384
PAGE_SIZE = 128

N_HEADS_A = 4
DK_A = D_MODEL // 16
DV_A = D_MODEL // 8
LOW_RANK = 16
GATE_TAU = 16.0
N_HEADS_R = 4
DK_R = D_MODEL // 16
DV_R = D_MODEL // 8
ROPE_BASE = 10000.0
QK_A = N_HEADS_A * DK_A
V_A = N_HEADS_A * DV_A
QK_R = N_HEADS_R * DK_R
V_R = N_HEADS_R * DV_R
D_FF = ((8 * D_MODEL // 3) + 127) // 128 * 128
D_PLE = 256
CHUNK = 64
EPS = 1e-6
IN_SPLITS = (QK_A, QK_A, V_A, V_A, QK_R, QK_R, V_R, V_R, LOW_RANK, D_MODEL, D_MODEL)
IN_COLS = sum(IN_SPLITS)

kernel_name = "hybrid_gla_retention_macaron_step"


def _split_points():
    return [int(c) for c in np.cumsum(IN_SPLITS)[:-1]]


def _ret_log_decay():
    h = jnp.arange(N_HEADS_R, dtype=jnp.float32)
    return jnp.log1p(-jnp.exp2(-5.0 - h))


def rmsnorm(x, w):
    xf = x.astype(jnp.float32)
    y = xf * lax.rsqrt(jnp.mean(xf * xf, axis=-1, keepdims=True) + EPS)
    return (y * w).astype(x.dtype)


def group_rmsnorm(o, w, dtype):
    B, T, H, dv = o.shape
    y = o * lax.rsqrt(jnp.mean(o * o, axis=-1, keepdims=True) + EPS)
    return (y * w.reshape(H, dv)).reshape(B, T, H * dv).astype(dtype)


def swiglu(x, w_in, w_out):
    a, b = jnp.split(x @ w_in, 2, axis=-1)
    return (jax.nn.silu(a) * b) @ w_out


def rotary(x, pos):
    d = x.shape[-1]
    half = d // 2
    freq = ROPE_BASE ** (-jnp.arange(half, dtype=jnp.float32) / half)
    ang = pos.astype(jnp.float32)[:, None] * freq[None, :]
    cos = jnp.cos(ang)[None, :, None, :]
    sin = jnp.sin(ang)[None, :, None, :]
    x1 = x[..., :half].astype(jnp.float32)
    x2 = x[..., half:].astype(jnp.float32)
    return jnp.concatenate([x1 * cos - x2 * sin, x1 * sin + x2 * cos], axis=-1).astype(x.dtype)


def gated_linear_scan(q, k, v, log_decay, s0, chunk):
    B, T, H, dk = q.shape
    dv = v.shape[-1]
    dg = log_decay.shape[-1]
    n = T // chunk

    def to_chunks(a):
        return a.astype(jnp.float32).reshape(B, n, chunk, H, a.shape[-1]).transpose(1, 0, 3, 2, 4)

    causal = jnp.tril(jnp.ones((chunk, chunk), dtype=bool))[:, :, None]

    def step(s, inp):
        qc, kc, vc, gc = inp
        b = jnp.cumsum(gc, axis=2)
        b_last = b[:, :, -1:, :]
        diff = b[:, :, :, None, :] - b[:, :, None, :, :]
        decay = jnp.exp(jnp.where(causal, diff, -jnp.inf))
        if dg == 1:
            scores = jnp.einsum('bhid,bhjd->bhij', qc, kc) * decay[..., 0]
        else:
            scores = jnp.einsum('bhid,bhjd,bhijd->bhij', qc, kc, decay)
        o = jnp.einsum('bhij,bhjv->bhiv', scores, vc) + jnp.einsum('bhid,bhdv->bhiv', qc * jnp.exp(b), s)
        s_new = jnp.exp(b_last)[:, :, 0, :, None] * s + jnp.einsum('bhjd,bhjv->bhdv', kc * jnp.exp(b_last - b), vc)
        return s_new, o

    s_final, o = lax.scan(step, s0.astype(jnp.float32),
                          (to_chunks(q), to_chunks(k), to_chunks(v), to_chunks(log_decay)))
    o = o.transpose(1, 0, 3, 2, 4).reshape(B, T, H, dv)
    return o, s_final


def trunk_layer(x, p, s_gla0, s_ret0, pos,
                norm_ffn1, w_ffn1_in, w_ffn1_out, norm_mix, w_in, w_alpha_up, b_alpha,
                gn_gla, gn_ret, w_out, norm_ffn2, w_ffn2_in, w_ffn2_out,
                norm_ple, w_ple_gate, w_ple_proj):
    B, T, _ = x.shape
    chunk = CHUNK if T % CHUNK == 0 else T
    h = x + 0.5 * swiglu(rmsnorm(x, norm_ffn1), w_ffn1_in, w_ffn1_out)
    u = rmsnorm(h, norm_mix)
    qa, ka, va, ra, qr, kr, vr, gr, a_low, gate_a, gate_r = jnp.split(u @ w_in, _split_points(), axis=-1)
    qa = qa.reshape(B, T, N_HEADS_A, DK_A) * (DK_A ** -0.5)
    ka = ka.reshape(B, T, N_HEADS_A, DK_A)
    va = va.reshape(B, T, N_HEADS_A, DV_A)
    log_alpha = jax.nn.log_sigmoid((a_low @ w_alpha_up + b_alpha).astype(jnp.float32)) / GATE_TAU
    log_alpha = log_alpha.reshape(B, T, N_HEADS_A, DK_A)
    oa, s_gla = gated_linear_scan(qa, ka, va, log_alpha, s_gla0, chunk)
    oa = group_rmsnorm(oa, gn_gla, x.dtype) * jax.nn.silu(ra)
    qr = rotary(qr.reshape(B, T, N_HEADS_R, DK_R), pos)
    kr = rotary(kr.reshape(B, T, N_HEADS_R, DK_R), pos) * (DK_R ** -0.5)
    vr = vr.reshape(B, T, N_HEADS_R, DV_R)
    log_gamma = jnp.broadcast_to(_ret_log_decay()[None, None, :, None], (B, T, N_HEADS_R, 1))
    orr, s_ret = gated_linear_scan(qr, kr, vr, log_gamma, s_ret0, chunk)
    orr = group_rmsnorm(orr, gn_ret, x.dtype) * jax.nn.silu(gr)
    mix = jax.nn.sigmoid(gate_a) * (oa @ w_out[:V_A]) + jax.nn.sigmoid(gate_r) * (orr @ w_out[V_A:])
    h = h + mix
    h = h + 0.5 * swiglu(rmsnorm(h, norm_ffn2), w_ffn2_in, w_ffn2_out)
    h = h + (p @ w_ple_proj) * jax.nn.sigmoid(rmsnorm(h, norm_ple) @ w_ple_gate)
    return h, s_gla, s_ret


def setup_inputs(seed: int = 0) -> dict:
    key = jax.random.key(seed)
    ks = jax.random.split(key, 24)
    f32 = jnp.float32
    nrm = lambda k, shape, s: jax.random.normal(k, shape, f32) * s
    gain = lambda k, n: 1.0 + 0.02 * jax.random.normal(k, (DEPTH, n), f32)
    return {
        "x_prompt": nrm(ks[0], (BATCH, SEQ, D_MODEL), 1.0),
        "x_sample": nrm(ks[1], (DEC_BATCH, DEC_SEQ, D_MODEL), 1.0),
        "state_gla": nrm(ks[2], (DEPTH, DEC_BATCH, N_HEADS_A, DK_A, DV_A), 0.5),
        "state_ret": nrm(ks[3], (DEPTH, DEC_BATCH, N_HEADS_R, DK_R, DV_R), 0.5),
        "p_prompt": nrm(ks[4], (DEPTH, BATCH, SEQ, D_PLE), 1.0),
        "p_sample": nrm(ks[5], (DEPTH, DEC_BATCH, DEC_SEQ, D_PLE), 1.0),
        "norm_ffn1": gain(ks[6], D_MODEL),
        "w_ffn1_in": nrm(ks[7], (DEPTH, D_MODEL, 2 * D_FF), D_MODEL ** -0.5),
        "w_ffn1_out": nrm(ks[8], (DEPTH, D_FF, D_MODEL), D_FF ** -0.5),
        "norm_mix": gain(ks[9], D_MODEL),
        "w_in": nrm(ks[10], (DEPTH, D_MODEL, IN_COLS), D_MODEL ** -0.5),
        "w_alpha_up": nrm(ks[11], (DEPTH, LOW_RANK, QK_A), LOW_RANK ** -0.5),
        "b_alpha": nrm(ks[12], (DEPTH, QK_A), 0.01),
        "gn_gla": gain(ks[13], V_A),
        "gn_ret": gain(ks[14], V_R),
        "w_out": nrm(ks[15], (DEPTH, V_A + V_R, D_MODEL), (V_A + V_R) ** -0.5),
        "norm_ffn2": gain(ks[16], D_MODEL),
        "w_ffn2_in": nrm(ks[17], (DEPTH, D_MODEL, 2 * D_FF), D_MODEL ** -0.5),
        "w_ffn2_out": nrm(ks[18], (DEPTH, D_FF, D_MODEL), D_FF ** -0.5),
        "norm_ple": gain(ks[19], D_MODEL),
        "w_ple_gate": nrm(ks[20], (DEPTH, D_MODEL, D_MODEL), D_MODEL ** -0.5),
        "w_ple_proj": nrm(ks[21], (DEPTH, D_PLE, D_MODEL), D_PLE ** -0.5),
        "norm_final": 1.0 + 0.02 * jax.random.normal(ks[22], (D_MODEL,), f32),
    }


def reference(x_prompt, x_sample, state_gla, state_ret, p_prompt, p_sample,
              norm_ffn1, w_ffn1_in, w_ffn1_out, norm_mix, w_in, w_alpha_up, b_alpha,
              gn_gla, gn_ret, w_out, norm_ffn2, w_ffn2_in, w_ffn2_out,
              norm_ple, w_ple_gate, w_ple_proj, norm_final):
    Bp, Tp, _ = x_prompt.shape
    Bs, Ts, _ = x_sample.shape
    pos_prompt = jnp.arange(Tp, dtype=jnp.int32)
    pos_sample = PAST_LEN + jnp.arange(Ts, dtype=jnp.int32)
    hp, hs = x_prompt, x_sample
    gla_p, ret_p, gla_s, ret_s = [], [], [], []
    for i in range(DEPTH):
        w = (norm_ffn1[i], w_ffn1_in[i], w_ffn1_out[i], norm_mix[i], w_in[i], w_alpha_up[i], b_alpha[i],
             gn_gla[i], gn_ret[i], w_out[i], norm_ffn2[i], w_ffn2_in[i], w_ffn2_out[i],
             norm_ple[i], w_ple_gate[i], w_ple_proj[i])
        zero_gla = jnp.zeros((Bp, N_HEADS_A, DK_A, DV_A), jnp.float32)
        zero_ret = jnp.zeros((Bp, N_HEADS_R, DK_R, DV_R), jnp.float32)
        hp, sa_p, sr_p = trunk_layer(hp, p_prompt[i], zero_gla, zero_ret, pos_prompt, *w)
        hs, sa_s, sr_s = trunk_layer(hs, p_sample[i], state_gla[i], state_ret[i], pos_sample, *w)
        gla_p.append(sa_p.astype(state_gla.dtype))
        ret_p.append(sr_p.astype(state_ret.dtype))
        gla_s.append(sa_s.astype(state_gla.dtype))
        ret_s.append(sr_s.astype(state_ret.dtype))
    y_prompt = rmsnorm(hp, norm_final)
    y_sample = rmsnorm(hs, norm_final)
    return (y_prompt, y_sample, jnp.stack(gla_p), jnp.stack(ret_p), jnp.stack(gla_s), jnp.stack(ret_s))
```

```python
import functools
import math

import jax
import jax.numpy as jnp
from jax import lax
from jax.experimental import pallas as pl
from jax.experimental.pallas import tpu as pltpu

F32 = jnp.float32
BF16 = jnp.bfloat16

N_HEADS = 4
DK = 64
DV = 128
QK = N_HEADS * DK
VW = N_HEADS * DV
LOW_RANK = 16
GATE_TAU = 16.0
ROPE_BASE = 10000.0
PAST_LEN = 16384
REF_CHUNK = 64
EPS = 1e-6
RET_LOG_DECAY = tuple(math.log1p(-(2.0 ** (-5.0 - h))) for h in range(N_HEADS))

TOKEN_TILE = 256
SCAN_TILE = 512
SCAN_CHUNK = 128
SAMPLE_SEQS = 8
VMEM_LIMIT = 56 * 1024 * 1024

NT = (((1,), (1,)), ((), ()))
TN = (((0,), (0,)), ((), ()))


def _dot(a, b):
    return jnp.dot(a, b, preferred_element_type=F32)


def _dot_nt(a, b):
    return lax.dot_general(a, b, NT, preferred_element_type=F32)


def _dot_tn(a, b):
    return lax.dot_general(a, b, TN, preferred_element_type=F32)


def _rms(x, w):
    return x * lax.rsqrt(jnp.mean(x * x, axis=-1, keepdims=True) + EPS) * w


def _sigmoid(x):
    return jax.nn.sigmoid(x)


def _ffn(u, wa_ref, wb_ref, wo_ref):
    d_ff = wa_ref.shape[1]
    half = d_ff // 2
    out = None
    for c in range(2):
        sl = slice(c * half, (c + 1) * half)
        a = _dot(u, wa_ref[:, sl])
        b = _dot(u, wb_ref[:, sl])
        g = (a * _sigmoid(a) * b).astype(BF16)
        part = _dot(g, wo_ref[sl, :])
        out = part if out is None else out + part
    return out


def _split3(x):
    hi = x.astype(BF16)
    r1 = x - hi.astype(F32)
    mid = r1.astype(BF16)
    lo = (r1 - mid.astype(F32)).astype(BF16)
    return hi, mid, lo


def _head_lane_const(values, width, per_head):
    lane = lax.broadcasted_iota(jnp.int32, (1, width), 1) // per_head
    out = jnp.full((1, width), values[-1], F32)
    for h in range(len(values) - 2, -1, -1):
        out = jnp.where(lane == h, values[h], out)
    return out


def _swap_halves(x):
    n = x.shape[-1]
    lane = lax.broadcasted_iota(jnp.int32, x.shape, 1)
    fwd = pltpu.roll(x, n - DK // 2, 1)
    bwd = pltpu.roll(x, DK // 2, 1)
    return jnp.where(lane % DK < DK // 2, fwd, bwd)


def _pre_kernel(x_ref, cos_ref, sin_ref, n1_ref, wa_ref, wb_ref, wo_ref, nm_ref,
                wmain_ref, walow_ref, wgate_ref, wup_ref, balpha_ref,
                h1_ref, qa_ref, ka_ref, va_ref, la_ref, qr_ref, kr_ref, vr_ref,
                ra_ref, gr_ref, ga_ref, gt_ref):
    x = x_ref[...]
    h = x + 0.5 * _ffn(_rms(x, n1_ref[...]).astype(BF16), wa_ref, wb_ref, wo_ref)
    h1_ref[...] = h
    u = _rms(h, nm_ref[...]).astype(BF16)

    def proj(lo, hi):
        return _dot(u, wmain_ref[:, lo:hi])

    o = 0
    qa_ref[...] = proj(o, o + QK) * (DK ** -0.5); o += QK
    ka_ref[...] = proj(o, o + QK); o += QK
    va_ref[...] = proj(o, o + VW).astype(BF16); o += VW
    ra = proj(o, o + VW); o += VW
    ra_ref[...] = (ra * _sigmoid(ra)).astype(BF16)
    cos = cos_ref[...]
    sin = sin_ref[...]
    qr = proj(o, o + QK); o += QK
    qr_ref[...] = qr * cos + _swap_halves(qr) * sin
    kr = proj(o, o + QK); o += QK
    kr_ref[...] = (kr * cos + _swap_halves(kr) * sin) * (DK ** -0.5)
    vr_ref[...] = proj(o, o + VW).astype(BF16); o += VW
    gr = proj(o, o + VW); o += VW
    gr_ref[...] = (gr * _sigmoid(gr)).astype(BF16)

    a_low = _dot(u, walow_ref[...])
    z = _dot(a_low.astype(BF16), wup_ref[...]) + balpha_ref[...]
    log_sig = jnp.minimum(z, 0.0) - jnp.log1p(jnp.exp(-jnp.abs(z)))
    la_ref[...] = log_sig * (1.0 / GATE_TAU)

    d = ga_ref.shape[1]
    gates = _dot(u, wgate_ref[...])
    ga_ref[...] = _sigmoid(gates[:, :d]).astype(BF16)
    gt_ref[...] = _sigmoid(gates[:, d:]).astype(BF16)


def _whole(_):
    return pl.BlockSpec(memory_space=pltpu.VMEM)


def _pre_call(x, cos, sin, weights):
    n, d = x.shape
    tm = TOKEN_TILE
    table_blocks = cos.shape[0] // tm
    row = lambda w: pl.BlockSpec((tm, w), lambda i: (i, 0))
    tab = pl.BlockSpec((tm, QK), lambda i: (i % table_blocks, 0))
    out_widths = [(d, F32), (QK, F32), (QK, F32), (VW, BF16), (QK, F32), (QK, F32), (QK, F32),
                  (VW, BF16), (VW, BF16), (VW, BF16), (d, BF16), (d, BF16)]
    return pl.pallas_call(
        _pre_kernel,
        grid=(n // tm,),
        in_specs=[row(d), tab, tab] + [_whole(w) for w in weights],
        out_specs=[row(w) for w, _ in out_widths],
        out_shape=[jax.ShapeDtypeStruct((n, w), t) for w, t in out_widths],
        compiler_params=pltpu.CompilerParams(
            dimension_semantics=("arbitrary",), vmem_limit_bytes=VMEM_LIMIT),
        name="pre",
    )(x, cos, sin, *weights)


def _prompt_scan_kernel(qa_ref, ka_ref, la_ref, va_ref, qr_ref, kr_ref, vr_ref,
                        o_ref, sg_ref, sr_ref, stg_ref, str_ref):
    t = pl.program_id(1)
    c_len = SCAN_CHUNK
    tile = qa_ref.shape[0]
    n_chunks = tile // c_len

    @pl.when(t == 0)
    def _():
        stg_ref[...] = jnp.zeros_like(stg_ref)
        str_ref[...] = jnp.zeros_like(str_ref)

    ri = lax.broadcasted_iota(jnp.int32, (tile, tile), 0)
    ci = lax.broadcasted_iota(jnp.int32, (tile, tile), 1)
    cum = jnp.where((ci <= ri) & (ri // c_len == ci // c_len), 1.0, 0.0).astype(BF16)
    hi, mid, lo = _split3(la_ref[...])
    b_all = _dot(cum, hi) + _dot(cum, mid) + _dot(cum, lo)

    r = lax.broadcasted_iota(jnp.int32, (c_len, c_len), 0)
    c = lax.broadcasted_iota(jnp.int32, (c_len, c_len), 1)
    causal = c <= r
    rowf = r.astype(F32)
    dist = (r - c).astype(F32)
    lg_lane = _head_lane_const(RET_LOG_DECAY, QK, DK)
    row_qk = lax.broadcasted_iota(jnp.int32, (c_len, QK), 0).astype(F32)
    k_decay = jnp.exp(lg_lane * (c_len - 1.0 - row_qk))
    chunk_decay = jnp.exp(lg_lane * float(c_len))

    for n in range(n_chunks):
        rows = slice(n * c_len, (n + 1) * c_len)
        b = b_all[rows]
        b_last = b[c_len - 1:c_len, :]
        k = ka_ref[rows, :]
        qt = (qa_ref[rows, :] * jnp.exp(b)).astype(BF16)
        kt = (k * jnp.exp(-b)).astype(BF16)
        kd = (k * jnp.exp(b_last - b)).astype(BF16)
        st = stg_ref[...]
        st_b = st.astype(BF16)
        stg_ref[...] = st * jnp.exp(b_last)
        for h in range(N_HEADS):
            ks = slice(h * DK, (h + 1) * DK)
            vs = slice(h * DV, (h + 1) * DV)
            v = va_ref[rows, vs]
            s = jnp.where(causal, _dot_nt(qt[:, ks], kt[:, ks]), 0.0).astype(BF16)
            o_ref[rows, vs] = _dot(s, v) + _dot_nt(qt[:, ks], st_b[:, ks])
            stg_ref[:, ks] += _dot_tn(v, kd[:, ks])
        q = qr_ref[rows, :].astype(BF16)
        kf = kr_ref[rows, :]
        k = kf.astype(BF16)
        kd = (kf * k_decay).astype(BF16)
        st = str_ref[...]
        st_b = st.astype(BF16)
        str_ref[...] = st * chunk_decay
        for h in range(N_HEADS):
            ks = slice(h * DK, (h + 1) * DK)
            vs = slice(VW + h * DV, VW + (h + 1) * DV)
            v = vr_ref[rows, h * DV:(h + 1) * DV]
            lg = RET_LOG_DECAY[h]
            s = (_dot_nt(q[:, ks], k[:, ks]) * jnp.where(causal, jnp.exp(lg * dist), 0.0)).astype(BF16)
            o_ref[rows, vs] = _dot(s, v) + jnp.exp(lg * (rowf + 1.0)) * _dot_nt(q[:, ks], st_b[:, ks])
            str_ref[:, ks] += _dot_tn(v, kd[:, ks])

    @pl.when(t == pl.num_programs(1) - 1)
    def _():
        for h in range(N_HEADS):
            ks = slice(h * DK, (h + 1) * DK)
            sg_ref[0, h] = stg_ref[:, ks].T
            sr_ref[0, h] = str_ref[:, ks].T


def _prompt_scan_call(qa, ka, la, va, qr, kr, vr, batch, seq):
    tile = SCAN_TILE
    steps = seq // tile
    row = lambda w: pl.BlockSpec((tile, w), lambda b, t: (b * steps + t, 0))
    state = pl.BlockSpec((1, N_HEADS, DK, DV), lambda b, t: (b, 0, 0, 0))
    state_shape = jax.ShapeDtypeStruct((batch, N_HEADS, DK, DV), F32)
    return pl.pallas_call(
        _prompt_scan_kernel,
        grid=(batch, steps),
        in_specs=[row(QK), row(QK), row(QK), row(VW), row(QK), row(QK), row(VW)],
        out_specs=[row(2 * VW), state, state],
        out_shape=[jax.ShapeDtypeStruct((batch * seq, 2 * VW), F32), state_shape, state_shape],
        scratch_shapes=[pltpu.VMEM((DV, QK), F32), pltpu.VMEM((DV, QK), F32)],
        compiler_params=pltpu.CompilerParams(
            dimension_semantics=("arbitrary", "arbitrary"), vmem_limit_bytes=VMEM_LIMIT),
        name="prompt_scan",
    )(qa, ka, la, va, qr, kr, vr)


def _sample_scan_kernel(qa_ref, ka_ref, la_ref, va_ref, qr_ref, kr_ref, vr_ref, sg0_ref, sr0_ref,
                        o_ref, sg_ref, sr_ref, *, steps):
    rows = qa_ref.shape[0]
    n_seq = rows // steps
    pair = 8 // steps
    t_qk = lax.broadcasted_iota(jnp.int32, (rows, QK), 0) % steps
    t_v = lax.broadcasted_iota(jnp.int32, (rows, VW), 0) % steps
    lane_head = lax.broadcasted_iota(jnp.int32, (QK, VW), 0) // DK
    col_head = lax.broadcasted_iota(jnp.int32, (QK, VW), 1) // DV
    seg = jnp.where(lane_head == col_head, 1.0, 0.0).astype(BF16)

    def down(x, d, t):
        return x if d == 0 else jnp.where(t >= d, pltpu.roll(x, d, 0), 0.0)

    g = la_ref[...]
    b = g
    for d in range(1, steps):
        b = b + down(g, d, t_qk)
    last = jnp.where(t_qk == steps - 1, b, 0.0)
    b_last = last
    for d in range(1, steps):
        b_last = b_last + pltpu.roll(last, rows - d, 0)

    qa = qa_ref[...]
    ka = ka_ref[...]
    va = va_ref[...].astype(F32)
    qr = qr_ref[...]
    kr = kr_ref[...]
    vr = vr_ref[...].astype(F32)
    lg_lane = _head_lane_const(RET_LOG_DECAY, QK, DK)
    lg_v = _head_lane_const(RET_LOG_DECAY, VW, DV)

    oa = jnp.zeros((rows, VW), F32)
    orr = jnp.zeros((rows, VW), F32)
    for d in range(steps):
        ok = t_qk >= d
        w = jnp.where(ok, jnp.exp(jnp.where(ok, b - down(b, d, t_qk), 0.0)), 0.0)
        pa = (qa * down(ka, d, t_qk) * w).astype(BF16)
        oa = oa + _dot(pa, seg) * down(va, d, t_v)
        pr = (qr * down(kr, d, t_qk) * jnp.where(ok, jnp.exp(lg_lane * float(d)), 0.0)).astype(BF16)
        orr = orr + _dot(pr, seg) * down(vr, d, t_v)

    qta = (qa * jnp.exp(b)).astype(BF16)
    kda = (ka * jnp.exp(b_last - b)).astype(BF16)
    e_hi, e_mid, e_lo = _split3(jnp.exp(b_last))
    zero = jnp.zeros_like(e_hi)
    e_parts = jnp.where(t_qk == 0, e_hi, jnp.where(t_qk == 1, e_mid, jnp.where(t_qk == 2, e_lo, zero)))
    qtr = qr.astype(BF16)
    q_scale = jnp.exp(lg_v * (t_v.astype(F32) + 1.0))
    kdr = (kr * jnp.exp(lg_lane * (steps - 1.0 - t_qk.astype(F32)))).astype(BF16)
    ones = jnp.ones((8, DV), BF16)
    slab_row = lax.broadcasted_iota(jnp.int32, (8, 1), 0) // steps
    va_b = va_ref[...]
    vr_b = vr_ref[...]

    for p in range(n_seq // pair):
        slab = slice(8 * p, 8 * p + 8)
        for h in range(N_HEADS):
            ks = slice(h * DK, (h + 1) * DK)
            vs = slice(h * DV, (h + 1) * DV)
            inter_a = jnp.zeros((8, DV), F32)
            inter_r = jnp.zeros((8, DV), F32)
            for j in range(pair):
                s_idx = p * pair + j
                mine = slab_row == j
                s0a = sg0_ref[s_idx, h]
                s0r = sr0_ref[s_idx, h]
                inter_a = jnp.where(mine, _dot(qta[slab, ks], s0a.astype(BF16)), inter_a)
                inter_r = jnp.where(mine, _dot(qtr[slab, ks], s0r.astype(BF16)), inter_r)
                e_col = _dot_tn(jnp.where(mine, e_parts[slab, ks], 0.0).astype(BF16), ones)
                kd = jnp.where(mine, kda[slab, ks], 0.0).astype(BF16)
                sg_ref[s_idx, h] = e_col * s0a + _dot_tn(kd, va_b[slab, vs])
                kd = jnp.where(mine, kdr[slab, ks], 0.0).astype(BF16)
                sr_ref[s_idx, h] = math.exp(RET_LOG_DECAY[h] * steps) * s0r + _dot_tn(kd, vr_b[slab, vs])
            o_ref[slab, vs] = oa[slab, vs] + inter_a
            o_ref[slab, VW + h * DV:VW + (h + 1) * DV] = orr[slab, vs] + q_scale[slab, vs] * inter_r


def _sample_scan_call(qa, ka, la, va, qr, kr, vr, sg0, sr0, steps):
    n_seq = sg0.shape[0]
    rows = SAMPLE_SEQS * steps
    row = lambda w: pl.BlockSpec((rows, w), lambda i: (i, 0))
    state = pl.BlockSpec((SAMPLE_SEQS, N_HEADS, DK, DV), lambda i: (i, 0, 0, 0))
    state_shape = jax.ShapeDtypeStruct(sg0.shape, F32)
    return pl.pallas_call(
        functools.partial(_sample_scan_kernel, steps=steps),
        grid=(n_seq // SAMPLE_SEQS,),
        in_specs=[row(QK), row(QK), row(QK), row(VW), row(QK), row(QK), row(VW), state, state],
        out_specs=[row(2 * VW), state, state],
        out_shape=[jax.ShapeDtypeStruct((n_seq * steps, 2 * VW), F32), state_shape, state_shape],
        compiler_params=pltpu.CompilerParams(
            dimension_semantics=("arbitrary",), vmem_limit_bytes=VMEM_LIMIT),
        name="sample_scan",
    )(qa, ka, la, va, qr, kr, vr, sg0, sr0)


def _post_kernel(o_ref, h1_ref, ra_ref, gr_ref, ga_ref, gt_ref, p_ref,
                 gn_ref, wout_ref, n2_ref, wa_ref, wb_ref, wo_ref, npl_ref, wpg_ref, wpp_ref, nf_ref,
                 y_ref, *, final):
    o = o_ref[...]
    normed = []
    for h in range(2 * N_HEADS):
        oh = o[:, h * DV:(h + 1) * DV]
        normed.append(oh * lax.rsqrt(jnp.mean(oh * oh, axis=-1, keepdims=True) + EPS))
    on = jnp.concatenate(normed, axis=-1) * gn_ref[...]
    oa = (on[:, :VW] * ra_ref[...].astype(F32)).astype(BF16)
    orr = (on[:, VW:] * gr_ref[...].astype(F32)).astype(BF16)
    mix = (ga_ref[...].astype(F32) * _dot(oa, wout_ref[:VW, :])
           + gt_ref[...].astype(F32) * _dot(orr, wout_ref[VW:, :]))
    h = h1_ref[...] + mix
    h = h + 0.5 * _ffn(_rms(h, n2_ref[...]).astype(BF16), wa_ref, wb_ref, wo_ref)
    gate = _dot(_rms(h, npl_ref[...]).astype(BF16), wpg_ref[...])
    h = h + _dot(p_ref[...].astype(BF16), wpp_ref[...]) * _sigmoid(gate)
    y_ref[...] = _rms(h, nf_ref[...]) if final else h


def _post_call(o, h1, ra, gr, ga, gt, p, weights, final):
    n, d = h1.shape
    tm = TOKEN_TILE
    row = lambda w: pl.BlockSpec((tm, w), lambda i: (i, 0))
    return pl.pallas_call(
        functools.partial(_post_kernel, final=final),
        grid=(n // tm,),
        in_specs=[row(2 * VW), row(d), row(VW), row(VW), row(d), row(d), row(p.shape[1])]
        + [_whole(w) for w in weights],
        out_specs=row(d),
        out_shape=jax.ShapeDtypeStruct((n, d), F32),
        compiler_params=pltpu.CompilerParams(
            dimension_semantics=("arbitrary",), vmem_limit_bytes=VMEM_LIMIT),
        name="post",
    )(o, h1, ra, gr, ga, gt, p, *weights)


def _rotary_tables(pos):
    half = DK // 2
    freq = ROPE_BASE ** (-jnp.arange(half, dtype=F32) / half)
    ang = pos.astype(F32)[:, None] * freq[None, :]
    cos, sin = jnp.cos(ang), jnp.sin(ang)
    return (jnp.tile(jnp.concatenate([cos, cos], axis=-1), (1, N_HEADS)),
            jnp.tile(jnp.concatenate([-sin, sin], axis=-1), (1, N_HEADS)))


def kernel(x_prompt, x_sample, state_gla, state_ret, p_prompt, p_sample, norm_ffn1, w_ffn1_in, w_ffn1_out, norm_mix, w_in, w_alpha_up, b_alpha, gn_gla, gn_ret, w_out, norm_ffn2, w_ffn2_in, w_ffn2_out, norm_ple, w_ple_gate, w_ple_proj, norm_final):
    bp, tp, d = x_prompt.shape
    bs, ts, _ = x_sample.shape
    depth = w_in.shape[0]
    d_ff = w_ffn1_out.shape[1]
    n_main = 4 * QK + 4 * VW
    assert tp % SCAN_TILE == 0 and tp % REF_CHUNK == 0 and ts < REF_CHUNK and 8 % ts == 0
    assert (bs * ts) % TOKEN_TILE == 0 and bs % SAMPLE_SEQS == 0

    cos_p, sin_p = _rotary_tables(jnp.arange(tp, dtype=jnp.int32))
    cos_s, sin_s = _rotary_tables(PAST_LEN + jnp.arange(ts, dtype=jnp.int32))
    cos_s = jnp.tile(cos_s, (TOKEN_TILE // ts, 1))
    sin_s = jnp.tile(sin_s, (TOKEN_TILE // ts, 1))

    hp = x_prompt.reshape(bp * tp, d)
    hs = x_sample.reshape(bs * ts, d)
    row = lambda v: v.reshape(1, -1).astype(F32)
    gla_p, ret_p, gla_s, ret_s = [], [], [], []
    for i in range(depth):
        pre_w = (row(norm_ffn1[i]), w_ffn1_in[i][:, :d_ff].astype(BF16), w_ffn1_in[i][:, d_ff:].astype(BF16),
                 w_ffn1_out[i].astype(BF16), row(norm_mix[i]),
                 w_in[i][:, :n_main].astype(BF16), w_in[i][:, n_main:n_main + LOW_RANK].astype(BF16),
                 w_in[i][:, n_main + LOW_RANK:].astype(BF16), w_alpha_up[i].astype(BF16), row(b_alpha[i]))
        post_w = (row(jnp.concatenate([gn_gla[i], gn_ret[i]])), w_out[i].astype(BF16), row(norm_ffn2[i]),
                  w_ffn2_in[i][:, :d_ff].astype(BF16), w_ffn2_in[i][:, d_ff:].astype(BF16),
                  w_ffn2_out[i].astype(BF16), row(norm_ple[i]), w_ple_gate[i].astype(BF16),
                  w_ple_proj[i].astype(BF16), row(norm_final))
        final = i == depth - 1

        h1, qa, ka, va, la, qr, kr, vr, ra, gr, ga, gt = _pre_call(hp, cos_p, sin_p, pre_w)
        o, sg, sr = _prompt_scan_call(qa, ka, la, va, qr, kr, vr, bp, tp)
        hp = _post_call(o, h1, ra, gr, ga, gt, p_prompt[i].reshape(bp * tp, -1), post_w, final)
        gla_p.append(sg.astype(state_gla.dtype))
        ret_p.append(sr.astype(state_ret.dtype))

        h1, qa, ka, va, la, qr, kr, vr, ra, gr, ga, gt = _pre_call(hs, cos_s, sin_s, pre_w)
        o, sg, sr = _sample_scan_call(qa, ka, la, va, qr, kr, vr,
                                      state_gla[i].astype(F32), state_ret[i].astype(F32), ts)
        hs = _post_call(o, h1, ra, gr, ga, gt, p_sample[i].reshape(bs * ts, -1), post_w, final)
        gla_s.append(sg.astype(state_gla.dtype))
        ret_s.append(sr.astype(state_ret.dtype))

    return (hp.reshape(bp, tp, d), hs.reshape(bs, ts, d),
            jnp.stack(gla_p), jnp.stack(ret_p), jnp.stack(gla_s), jnp.stack(ret_s))
```

```python
import functools
import math

import numpy as np

import jax
import jax.numpy as jnp
from jax import lax
from jax.experimental import pallas as pl
from jax.experimental.pallas import tpu as pltpu

F32 = jnp.float32
BF16 = jnp.bfloat16

N_HEADS = 4
DK = 64
DV = 128
QK = N_HEADS * DK
VW = N_HEADS * DV
N_MAIN = 4 * QK + 4 * VW
LOW_RANK = 16
GATE_TAU = 16.0
ROPE_BASE = 10000.0
PAST_LEN = 16384
REF_CHUNK = 64
EPS = 1e-6
RET_LOG_DECAY = tuple(math.log1p(-(2.0 ** (-5.0 - h))) for h in range(N_HEADS))

PRE_TILE = 256
POST_TILE = 512
SCAN_TILE = 512
SCAN_CHUNK = 128
SAMPLE_SEQS = 8
BF16_ROWS = 16
VMEM_LIMIT = 56 * 1024 * 1024
GLA_SAFE_LOG_DECAY = -60.0

NT = (((1,), (1,)), ((), ()))
TN = (((0,), (0,)), ((), ()))


def _dot(a, b):
    return jnp.dot(a, b, preferred_element_type=F32)


def _dot_nt(a, b):
    return lax.dot_general(a, b, NT, preferred_element_type=F32)


def _dot_tn(a, b):
    return lax.dot_general(a, b, TN, preferred_element_type=F32)


def _rms(x, w):
    return x * lax.rsqrt(jnp.mean(x * x, axis=-1, keepdims=True) + EPS) * w


def _sigmoid(x):
    return jax.nn.sigmoid(x)


def _ffn(u, wi_ref, wo_ref):
    d_ff = wo_ref.shape[0]
    half = d_ff // 2
    out = None
    for c in range(2):
        lo, hi = c * half, (c + 1) * half
        a = _dot(u, wi_ref[:, lo:hi])
        b = _dot(u, wi_ref[:, d_ff + lo:d_ff + hi])
        g = (a * _sigmoid(a) * b).astype(BF16)
        part = _dot(g, wo_ref[lo:hi, :])
        out = part if out is None else out + part
    return out


def _split3(x):
    hi = x.astype(BF16)
    r1 = x - hi.astype(F32)
    mid = r1.astype(BF16)
    lo = (r1 - mid.astype(F32)).astype(BF16)
    return hi, mid, lo


def _head_lane_const(values, width, per_head):
    lane = lax.broadcasted_iota(jnp.int32, (1, width), 1) // per_head
    out = jnp.full((1, width), values[-1], F32)
    for h in range(len(values) - 2, -1, -1):
        out = jnp.where(lane == h, values[h], out)
    return out


def _head_segment_sum():
    lane_head = lax.broadcasted_iota(jnp.int32, (QK, VW), 0) // DK
    col_head = lax.broadcasted_iota(jnp.int32, (QK, VW), 1) // DV
    return jnp.where(lane_head == col_head, 1.0, 0.0).astype(BF16)


def _swap_halves(x):
    n = x.shape[-1]
    lane = lax.broadcasted_iota(jnp.int32, x.shape, 1)
    fwd = pltpu.roll(x, n - DK // 2, 1)
    bwd = pltpu.roll(x, DK // 2, 1)
    return jnp.where(lane % DK < DK // 2, fwd, bwd)


def _whole(_):
    return pl.BlockSpec(memory_space=pltpu.VMEM)


def _pre_kernel(x_ref, cos_ref, sin_ref, n1_ref, w1i_ref, w1o_ref, nm_ref, win_ref, wup_ref, balpha_ref,
                *rest, n_cast):
    cast_in = rest[:n_cast]
    (h1_ref, qa_ref, ka_ref, va_ref, la_ref, qr_ref, kr_ref, vr_ref,
     ra_ref, gr_ref, ga_ref, gt_ref) = rest[n_cast:n_cast + 12]
    cast_out = rest[n_cast + 12:2 * n_cast + 12]
    wgate_ref = rest[-1]

    @pl.when(pl.program_id(0) == 0)
    def _():
        wgate_ref[...] = win_ref[:, N_MAIN + LOW_RANK:]

    for src, dst in zip(cast_in, cast_out):
        dst[...] = src[...].astype(BF16)

    x = x_ref[...]
    h = x + 0.5 * _ffn(_rms(x, n1_ref[...]).astype(BF16), w1i_ref, w1o_ref)
    h1_ref[...] = h
    u = _rms(h, nm_ref[...]).astype(BF16)

    def proj(lo, hi):
        return _dot(u, win_ref[:, lo:hi])

    o = 0
    qa_ref[...] = proj(o, o + QK) * (DK ** -0.5); o += QK
    ka_ref[...] = proj(o, o + QK); o += QK
    va_ref[...] = proj(o, o + VW).astype(BF16); o += VW
    ra = proj(o, o + VW); o += VW
    ra_ref[...] = (ra * _sigmoid(ra)).astype(BF16)
    cos = cos_ref[...]
    sin = sin_ref[...]
    qr = proj(o, o + QK); o += QK
    qr_ref[...] = qr * cos + _swap_halves(qr) * sin
    kr = proj(o, o + QK); o += QK
    kr_ref[...] = (kr * cos + _swap_halves(kr) * sin) * (DK ** -0.5)
    vr_ref[...] = proj(o, o + VW).astype(BF16); o += VW
    gr = proj(o, o + VW); o += VW
    gr_ref[...] = (gr * _sigmoid(gr)).astype(BF16)

    a_low = proj(N_MAIN, N_MAIN + LOW_RANK)
    z = _dot(a_low.astype(BF16), wup_ref[...]) + balpha_ref[...]
    log_sig = jnp.minimum(z, 0.0) - jnp.log1p(jnp.exp(-jnp.abs(z)))
    la_ref[...] = log_sig * (1.0 / GATE_TAU)

    d = ga_ref.shape[1]
    gates = _dot(u, wgate_ref[...])
    ga_ref[...] = _sigmoid(gates[:, :d]).astype(BF16)
    gt_ref[...] = _sigmoid(gates[:, d:]).astype(BF16)


def _cast_blocks(rows, steps):
    return max(n for n in range(1, steps + 1) if rows % n == 0 and (rows // n) % BF16_ROWS == 0)


def _pre_call(x, cos, sin, weights, to_cast):
    n, d = x.shape
    tm = PRE_TILE
    steps = n // tm
    table_blocks = cos.shape[0] // tm
    row = lambda w: pl.BlockSpec((tm, w), lambda i: (i, 0))
    tab = pl.BlockSpec((tm, QK), lambda i: (i % table_blocks, 0))
    out_widths = [(d, F32), (QK, F32), (QK, F32), (VW, BF16), (QK, F32), (QK, F32), (QK, F32),
                  (VW, BF16), (VW, BF16), (VW, BF16), (d, BF16), (d, BF16)]

    def slab(w):
        blocks = _cast_blocks(w.shape[0], steps)
        return pl.BlockSpec((w.shape[0] // blocks, w.shape[1]), lambda i: (jnp.minimum(i, blocks - 1), 0))

    cast_specs = [slab(w) for w in to_cast]
    outs = pl.pallas_call(
        functools.partial(_pre_kernel, n_cast=len(to_cast)),
        grid=(steps,),
        in_specs=[row(d), tab, tab] + [_whole(w) for w in weights] + cast_specs,
        out_specs=[row(w) for w, _ in out_widths] + cast_specs,
        out_shape=[jax.ShapeDtypeStruct((n, w), t) for w, t in out_widths]
        + [jax.ShapeDtypeStruct(w.shape, BF16) for w in to_cast],
        scratch_shapes=[pltpu.VMEM((d, 2 * d), BF16)],
        compiler_params=pltpu.CompilerParams(
            dimension_semantics=("arbitrary",), vmem_limit_bytes=VMEM_LIMIT),
        name="pre",
    )(x, cos, sin, *weights, *to_cast)
    return outs[:12], outs[12:]


def _prompt_scan_kernel(qa_ref, ka_ref, la_ref, va_ref, qr_ref, kr_ref, vr_ref,
                        o_ref, sg_ref, sr_ref,
                        stg_ref, str_ref, b_ref, acc_ref, ks_ref, bs_ref, vs_ref):
    t = pl.program_id(1)
    c_len = SCAN_CHUNK
    tile = qa_ref.shape[0]
    n_chunks = tile // c_len

    @pl.when(t == 0)
    def _():
        stg_ref[...] = jnp.zeros_like(stg_ref)
        str_ref[...] = jnp.zeros_like(str_ref)

    ri = lax.broadcasted_iota(jnp.int32, (tile, tile), 0)
    ci = lax.broadcasted_iota(jnp.int32, (tile, tile), 1)
    cum = jnp.where((ci <= ri) & (ri // c_len == ci // c_len), 1.0, 0.0).astype(BF16)
    hi, mid, lo = _split3(la_ref[...])
    b_all = _dot(cum, hi) + _dot(cum, mid) + _dot(cum, lo)
    b_ref[...] = b_all
    safe = jnp.min(b_all) >= GLA_SAFE_LOG_DECAY

    r = lax.broadcasted_iota(jnp.int32, (c_len, c_len), 0)
    c = lax.broadcasted_iota(jnp.int32, (c_len, c_len), 1)
    causal = c <= r
    rowf = r.astype(F32)
    dist = (r - c).astype(F32)

    def gla_intra_by_offsets():
        seg = _head_segment_sum()
        in_chunk = lax.broadcasted_iota(jnp.int32, (tile, QK), 0) % c_len
        ks_ref[...] = ka_ref[...]
        bs_ref[...] = b_ref[...]
        vs_ref[...] = va_ref[...].astype(F32)
        acc_ref[...] = jnp.zeros_like(acc_ref)

        def body(d, carry):
            ok = in_chunk >= d
            k_sh, b_sh, v_sh = ks_ref[...], bs_ref[...], vs_ref[...]
            w = jnp.exp(jnp.where(ok, b_ref[...] - b_sh, 0.0))
            p = jnp.where(ok, qa_ref[...] * k_sh * w, 0.0).astype(BF16)
            acc_ref[...] += _dot(p, seg) * v_sh
            ks_ref[...] = pltpu.roll(k_sh, 1, 0)
            bs_ref[...] = pltpu.roll(b_sh, 1, 0)
            vs_ref[...] = pltpu.roll(v_sh, 1, 0)
            return carry

        lax.fori_loop(0, c_len, body, 0)

    def gla(factored):
        for n in range(n_chunks):
            rows = slice(n * c_len, (n + 1) * c_len)
            b = b_ref[rows, :]
            b_last = b[c_len - 1:c_len, :]
            k = ka_ref[rows, :]
            qt = (qa_ref[rows, :] * jnp.exp(b)).astype(BF16)
            kd = (k * jnp.exp(b_last - b)).astype(BF16)
            if factored:
                kt = (k * jnp.exp(-b)).astype(BF16)
            st = stg_ref[...]
            st_b = st.astype(BF16)
            stg_ref[...] = st * jnp.exp(b_last)
            for h in range(N_HEADS):
                ks = slice(h * DK, (h + 1) * DK)
                vs = slice(h * DV, (h + 1) * DV)
                v = va_ref[rows, vs]
                if factored:
                    s = jnp.where(causal, _dot_nt(qt[:, ks], kt[:, ks]), 0.0).astype(BF16)
                    intra = _dot(s, v)
                else:
                    intra = acc_ref[rows, vs]
                o_ref[rows, vs] = intra + _dot_nt(qt[:, ks], st_b[:, ks])
                stg_ref[:, ks] += _dot_tn(v, kd[:, ks])

    def retention():
        lg_lane = _head_lane_const(RET_LOG_DECAY, QK, DK)
        row_qk = lax.broadcasted_iota(jnp.int32, (c_len, QK), 0).astype(F32)
        k_decay = jnp.exp(lg_lane * (c_len - 1.0 - row_qk))
        chunk_decay = jnp.exp(lg_lane * float(c_len))
        for n in range(n_chunks):
            rows = slice(n * c_len, (n + 1) * c_len)
            q = qr_ref[rows, :].astype(BF16)
            kf = kr_ref[rows, :]
            k = kf.astype(BF16)
            kd = (kf * k_decay).astype(BF16)
            st = str_ref[...]
            st_b = st.astype(BF16)
            str_ref[...] = st * chunk_decay
            for h in range(N_HEADS):
                ks = slice(h * DK, (h + 1) * DK)
                vs = slice(VW + h * DV, VW + (h + 1) * DV)
                v = vr_ref[rows, h * DV:(h + 1) * DV]
                lg = RET_LOG_DECAY[h]
                s = (_dot_nt(q[:, ks], k[:, ks]) * jnp.where(causal, jnp.exp(lg * dist), 0.0)).astype(BF16)
                o_ref[rows, vs] = _dot(s, v) + jnp.exp(lg * (rowf + 1.0)) * _dot_nt(q[:, ks], st_b[:, ks])
                str_ref[:, ks] += _dot_tn(v, kd[:, ks])

    @pl.when(safe)
    def _():
        gla(factored=True)
        retention()

    @pl.when(jnp.logical_not(safe))
    def _():
        gla_intra_by_offsets()
        gla(factored=False)
        retention()

    @pl.when(t == pl.num_programs(1) - 1)
    def _():
        for h in range(N_HEADS):
            ks = slice(h * DK, (h + 1) * DK)
            sg_ref[0, h] = stg_ref[:, ks].T
            sr_ref[0, h] = str_ref[:, ks].T


def _prompt_scan_call(qa, ka, la, va, qr, kr, vr, batch, seq):
    tile = SCAN_TILE
    steps = seq // tile
    row = lambda w: pl.BlockSpec((tile, w), lambda b, t: (b * steps + t, 0))
    state = pl.BlockSpec((1, N_HEADS, DK, DV), lambda b, t: (b, 0, 0, 0))
    state_shape = jax.ShapeDtypeStruct((batch, N_HEADS, DK, DV), F32)
    return pl.pallas_call(
        _prompt_scan_kernel,
        grid=(batch, steps),
        in_specs=[row(QK), row(QK), row(QK), row(VW), row(QK), row(QK), row(VW)],
        out_specs=[row(2 * VW), state, state],
        out_shape=[jax.ShapeDtypeStruct((batch * seq, 2 * VW), F32), state_shape, state_shape],
        scratch_shapes=[pltpu.VMEM((DV, QK), F32), pltpu.VMEM((DV, QK), F32),
                        pltpu.VMEM((tile, QK), F32), pltpu.VMEM((tile, VW), F32),
                        pltpu.VMEM((tile, QK), F32), pltpu.VMEM((tile, QK), F32), pltpu.VMEM((tile, VW), F32)],
        compiler_params=pltpu.CompilerParams(
            dimension_semantics=("arbitrary", "arbitrary"), vmem_limit_bytes=VMEM_LIMIT),
        name="prompt_scan",
    )(qa, ka, la, va, qr, kr, vr)


def _sample_scan_kernel(qa_ref, ka_ref, la_ref, va_ref, qr_ref, kr_ref, vr_ref, sg0_ref, sr0_ref,
                        o_ref, sg_ref, sr_ref, *, steps):
    rows = qa_ref.shape[0]
    n_seq = rows // steps
    pair = 8 // steps
    t_qk = lax.broadcasted_iota(jnp.int32, (rows, QK), 0) % steps
    t_v = lax.broadcasted_iota(jnp.int32, (rows, VW), 0) % steps
    seg = _head_segment_sum()

    def down(x, d, t):
        return x if d == 0 else jnp.where(t >= d, pltpu.roll(x, d, 0), 0.0)

    g = la_ref[...]
    b = g
    for d in range(1, steps):
        b = b + down(g, d, t_qk)
    last = jnp.where(t_qk == steps - 1, b, 0.0)
    b_last = last
    for d in range(1, steps):
        b_last = b_last + pltpu.roll(last, rows - d, 0)

    qa = qa_ref[...]
    ka = ka_ref[...]
    va = va_ref[...].astype(F32)
    qr = qr_ref[...]
    kr = kr_ref[...]
    vr = vr_ref[...].astype(F32)
    lg_lane = _head_lane_const(RET_LOG_DECAY, QK, DK)
    lg_v = _head_lane_const(RET_LOG_DECAY, VW, DV)

    oa = jnp.zeros((rows, VW), F32)
    orr = jnp.zeros((rows, VW), F32)
    for d in range(steps):
        ok = t_qk >= d
        w = jnp.where(ok, jnp.exp(jnp.where(ok, b - down(b, d, t_qk), 0.0)), 0.0)
        pa = (qa * down(ka, d, t_qk) * w).astype(BF16)
        oa = oa + _dot(pa, seg) * down(va, d, t_v)
        pr = (qr * down(kr, d, t_qk) * jnp.where(ok, jnp.exp(lg_lane * float(d)), 0.0)).astype(BF16)
        orr = orr + _dot(pr, seg) * down(vr, d, t_v)

    qta = (qa * jnp.exp(b)).astype(BF16)
    kda = (ka * jnp.exp(b_last - b)).astype(BF16)
    e_hi, e_mid, e_lo = _split3(jnp.exp(b_last))
    zero = jnp.zeros_like(e_hi)
    e_parts = jnp.where(t_qk == 0, e_hi, jnp.where(t_qk == 1, e_mid, jnp.where(t_qk == 2, e_lo, zero)))
    qtr = qr.astype(BF16)
    q_scale = jnp.exp(lg_v * (t_v.astype(F32) + 1.0))
    kdr = (kr * jnp.exp(lg_lane * (steps - 1.0 - t_qk.astype(F32)))).astype(BF16)
    ones = jnp.ones((8, DV), BF16)
    slab_row = lax.broadcasted_iota(jnp.int32, (8, 1), 0) // steps
    va_b = va_ref[...]
    vr_b = vr_ref[...]

    for p in range(n_seq // pair):
        slab = slice(8 * p, 8 * p + 8)
        for h in range(N_HEADS):
            ks = slice(h * DK, (h + 1) * DK)
            vs = slice(h * DV, (h + 1) * DV)
            inter_a = jnp.zeros((8, DV), F32)
            inter_r = jnp.zeros((8, DV), F32)
            for j in range(pair):
                s_idx = p * pair + j
                mine = slab_row == j
                s0a = sg0_ref[s_idx, h]
                s0r = sr0_ref[s_idx, h]
                inter_a = jnp.where(mine, _dot(qta[slab, ks], s0a.astype(BF16)), inter_a)
                inter_r = jnp.where(mine, _dot(qtr[slab, ks], s0r.astype(BF16)), inter_r)
                e_col = _dot_tn(jnp.where(mine, e_parts[slab, ks], 0.0).astype(BF16), ones)
                kd = jnp.where(mine, kda[slab, ks], 0.0).astype(BF16)
                sg_ref[s_idx, h] = e_col * s0a + _dot_tn(kd, va_b[slab, vs])
                kd = jnp.where(mine, kdr[slab, ks], 0.0).astype(BF16)
                sr_ref[s_idx, h] = math.exp(RET_LOG_DECAY[h] * steps) * s0r + _dot_tn(kd, vr_b[slab, vs])
            o_ref[slab, vs] = oa[slab, vs] + inter_a
            o_ref[slab, VW + h * DV:VW + (h + 1) * DV] = orr[slab, vs] + q_scale[slab, vs] * inter_r


def _sample_scan_call(qa, ka, la, va, qr, kr, vr, sg0, sr0, steps):
    n_seq = sg0.shape[0]
    rows = SAMPLE_SEQS * steps
    row = lambda w: pl.BlockSpec((rows, w), lambda i: (i, 0))
    state = pl.BlockSpec((SAMPLE_SEQS, N_HEADS, DK, DV), lambda i: (i, 0, 0, 0))
    state_shape = jax.ShapeDtypeStruct(sg0.shape, F32)
    return pl.pallas_call(
        functools.partial(_sample_scan_kernel, steps=steps),
        grid=(n_seq // SAMPLE_SEQS,),
        in_specs=[row(QK), row(QK), row(QK), row(VW), row(QK), row(QK), row(VW), state, state],
        out_specs=[row(2 * VW), state, state],
        out_shape=[jax.ShapeDtypeStruct((n_seq * steps, 2 * VW), F32), state_shape, state_shape],
        compiler_params=pltpu.CompilerParams(
            dimension_semantics=("arbitrary",), vmem_limit_bytes=VMEM_LIMIT),
        name="sample_scan",
    )(qa, ka, la, va, qr, kr, vr, sg0, sr0)


def _post_kernel(o_ref, h1_ref, ra_ref, gr_ref, ga_ref, gt_ref, p_ref,
                 gn_ref, wout_ref, n2_ref, w2i_ref, w2o_ref, npl_ref, wpg_ref, wpp_ref, nf_ref,
                 y_ref, *, final):
    o = o_ref[...]
    normed = []
    for h in range(2 * N_HEADS):
        oh = o[:, h * DV:(h + 1) * DV]
        normed.append(oh * lax.rsqrt(jnp.mean(oh * oh, axis=-1, keepdims=True) + EPS))
    on = jnp.concatenate(normed, axis=-1) * gn_ref[...]
    oa = (on[:, :VW] * ra_ref[...].astype(F32)).astype(BF16)
    orr = (on[:, VW:] * gr_ref[...].astype(F32)).astype(BF16)
    mix = (ga_ref[...].astype(F32) * _dot(oa, wout_ref[:VW, :])
           + gt_ref[...].astype(F32) * _dot(orr, wout_ref[VW:, :]))
    h = h1_ref[...] + mix
    h = h + 0.5 * _ffn(_rms(h, n2_ref[...]).astype(BF16), w2i_ref, w2o_ref)
    gate = _dot(_rms(h, npl_ref[...]).astype(BF16), wpg_ref[...])
    h = h + _dot(p_ref[...].astype(BF16), wpp_ref[...]) * _sigmoid(gate)
    y_ref[...] = _rms(h, nf_ref[...]) if final else h


def _post_call(o, h1, ra, gr, ga, gt, p, weights, final):
    n, d = h1.shape
    tm = POST_TILE
    row = lambda w: pl.BlockSpec((tm, w), lambda i: (i, 0))
    return pl.pallas_call(
        functools.partial(_post_kernel, final=final),
        grid=(n // tm,),
        in_specs=[row(2 * VW), row(d), row(VW), row(VW), row(d), row(d), row(p.shape[1])]
        + [_whole(w) for w in weights],
        out_specs=row(d),
        out_shape=jax.ShapeDtypeStruct((n, d), F32),
        compiler_params=pltpu.CompilerParams(
            dimension_semantics=("arbitrary",), vmem_limit_bytes=VMEM_LIMIT),
        name="post",
    )(o, h1, ra, gr, ga, gt, p, *weights)


def _rotary_tables(pos, rows):
    half = DK // 2
    freq = ROPE_BASE ** (-np.arange(half, dtype=np.float64) / half)
    ang = np.asarray(pos, np.float64)[:, None] * freq[None, :]
    cos, sin = np.cos(ang), np.sin(ang)
    reps = (rows // len(pos), N_HEADS)
    return (jnp.asarray(np.tile(np.concatenate([cos, cos], axis=-1), reps), F32),
            jnp.asarray(np.tile(np.concatenate([-sin, sin], axis=-1), reps), F32))


def kernel(x_prompt, x_sample, state_gla, state_ret, p_prompt, p_sample, norm_ffn1, w_ffn1_in, w_ffn1_out, norm_mix, w_in, w_alpha_up, b_alpha, gn_gla, gn_ret, w_out, norm_ffn2, w_ffn2_in, w_ffn2_out, norm_ple, w_ple_gate, w_ple_proj, norm_final):
    bp, tp, d = x_prompt.shape
    bs, ts, _ = x_sample.shape
    depth = w_in.shape[0]
    assert w_in.shape[2] == N_MAIN + LOW_RANK + 2 * d
    assert tp % SCAN_TILE == 0 and tp % REF_CHUNK == 0 and 3 <= ts < REF_CHUNK and 8 % ts == 0
    assert (bs * ts) % POST_TILE == 0 and (bs * ts) % PRE_TILE == 0 and bs % SAMPLE_SEQS == 0

    cos_p, sin_p = _rotary_tables(np.arange(tp), tp)
    cos_s, sin_s = _rotary_tables(PAST_LEN + np.arange(ts), PRE_TILE)

    hp = x_prompt.reshape(bp * tp, d)
    hs = x_sample.reshape(bs * ts, d)
    row = lambda v: v.reshape(1, -1).astype(F32)
    gla_p, ret_p, gla_s, ret_s = [], [], [], []
    for i in range(depth):
        pre_w = (row(norm_ffn1[i]), w_ffn1_in[i].astype(BF16), w_ffn1_out[i].astype(BF16), row(norm_mix[i]),
                 w_in[i].astype(BF16), w_alpha_up[i].astype(BF16), row(b_alpha[i]))
        post_cast = (w_out[i], w_ffn2_in[i], w_ffn2_out[i], w_ple_gate[i], w_ple_proj[i])
        final = i == depth - 1

        (h1, qa, ka, va, la, qr, kr, vr, ra, gr, ga, gt), post_bf = _pre_call(hp, cos_p, sin_p, pre_w, post_cast)
        wout_b, w2i_b, w2o_b, wpg_b, wpp_b = post_bf
        post_w = (row(jnp.concatenate([gn_gla[i], gn_ret[i]])), wout_b, row(norm_ffn2[i]), w2i_b, w2o_b,
                  row(norm_ple[i]), wpg_b, wpp_b, row(norm_final))
        o, sg, sr = _prompt_scan_call(qa, ka, la, va, qr, kr, vr, bp, tp)
        hp = _post_call(o, h1, ra, gr, ga, gt, p_prompt[i].reshape(bp * tp, -1), post_w, final)
        gla_p.append(sg.astype(state_gla.dtype))
        ret_p.append(sr.astype(state_ret.dtype))

        (h1, qa, ka, va, la, qr, kr, vr, ra, gr, ga, gt), _ = _pre_call(hs, cos_s, sin_s, pre_w, ())
        o, sg, sr = _sample_scan_call(qa, ka, la, va, qr, kr, vr,
                                      state_gla[i].astype(F32), state_ret[i].astype(F32), ts)
        hs = _post_call(o, h1, ra, gr, ga, gt, p_sample[i].reshape(bs * ts, -1), post_w, final)
        gla_s.append(sg.astype(state_gla.dtype))
        ret_s.append(sr.astype(state_ret.dtype))

    return (hp.reshape(bp, tp, d), hs.reshape(bs, ts, d),
            jnp.stack(gla_p), jnp.stack(ret_p), jnp.stack(gla_s), jnp.stack(ret_s))
```

```python
import functools
import math

import numpy as np

import jax
import jax.numpy as jnp
from jax import lax
from jax.experimental import pallas as pl
from jax.experimental.pallas import tpu as pltpu

F32 = jnp.float32
BF16 = jnp.bfloat16

N_HEADS = 4
DK = 64
DV = 128
QK = N_HEADS * DK
VW = N_HEADS * DV
N_MAIN = 4 * QK + 4 * VW
LOW_RANK = 16
GATE_TAU = 16.0
ROPE_BASE = 10000.0
PAST_LEN = 16384
REF_CHUNK = 64
EPS = 1e-6
RET_LOG_DECAY = tuple(math.log1p(-(2.0 ** (-5.0 - h))) for h in range(N_HEADS))

PRE_TILE = 256
POST_TILE = 512
SCAN_TILE = 512
SCAN_CHUNK = 128
SAMPLE_SEQS = 8
BF16_ROWS = 16
MXU_TILE = 256
VMEM_LIMIT = 56 * 1024 * 1024
GLA_SAFE_LOG_DECAY = -60.0

NT = (((1,), (1,)), ((), ()))
TN = (((0,), (0,)), ((), ()))


def _dot(a, b):
    return jnp.dot(a, b, preferred_element_type=F32)


def _dot_nt(a, b):
    return lax.dot_general(a, b, NT, preferred_element_type=F32)


def _dot_tn(a, b):
    return lax.dot_general(a, b, TN, preferred_element_type=F32)


def _rms(x, w):
    return x * lax.rsqrt(jnp.mean(x * x, axis=-1, keepdims=True) + EPS) * w


def _sigmoid(x):
    return jax.nn.sigmoid(x)


def _ffn(u, wi_ref, wo_ref):
    d_ff = wo_ref.shape[0]
    cut = -(-d_ff // (2 * MXU_TILE)) * MXU_TILE
    out = None
    for lo, hi in ((0, cut), (cut, d_ff)):
        a = _dot(u, wi_ref[:, lo:hi])
        b = _dot(u, wi_ref[:, d_ff + lo:d_ff + hi])
        g = (a * _sigmoid(a) * b).astype(BF16)
        part = _dot(g, wo_ref[lo:hi, :])
        out = part if out is None else out + part
    return out


def _split3(x):
    hi = x.astype(BF16)
    r1 = x - hi.astype(F32)
    mid = r1.astype(BF16)
    lo = (r1 - mid.astype(F32)).astype(BF16)
    return hi, mid, lo


def _head_lane_const(values, width, per_head):
    lane = lax.broadcasted_iota(jnp.int32, (1, width), 1) // per_head
    out = jnp.full((1, width), values[-1], F32)
    for h in range(len(values) - 2, -1, -1):
        out = jnp.where(lane == h, values[h], out)
    return out


def _head_segment_sum():
    lane_head = lax.broadcasted_iota(jnp.int32, (QK, VW), 0) // DK
    col_head = lax.broadcasted_iota(jnp.int32, (QK, VW), 1) // DV
    return jnp.where(lane_head == col_head, 1.0, 0.0).astype(BF16)


def _swap_halves(x):
    n = x.shape[-1]
    lane = lax.broadcasted_iota(jnp.int32, x.shape, 1)
    fwd = pltpu.roll(x, n - DK // 2, 1)
    bwd = pltpu.roll(x, DK // 2, 1)
    return jnp.where(lane % DK < DK // 2, fwd, bwd)


def _whole(_):
    return pl.BlockSpec(memory_space=pltpu.VMEM)


def _pre_kernel(x_ref, cos_ref, sin_ref, n1_ref, w1i_ref, w1o_ref, nm_ref, win_ref, wup_ref, balpha_ref,
                *rest, n_cast):
    cast_in = rest[:n_cast]
    (h1_ref, qa_ref, ka_ref, va_ref, la_ref, qr_ref, kr_ref, vr_ref,
     ra_ref, gr_ref, ga_ref, gt_ref) = rest[n_cast:n_cast + 12]
    cast_out = rest[n_cast + 12:2 * n_cast + 12]
    wgate_ref = rest[-1]

    @pl.when(pl.program_id(0) == 0)
    def _():
        wgate_ref[...] = win_ref[:, N_MAIN + LOW_RANK:]

    for src, dst in zip(cast_in, cast_out):
        dst[...] = src[...].astype(BF16)

    def chain(rows):
        x = x_ref[rows, :]
        u = _rms(x, n1_ref[...]).astype(BF16)
        yield
        f = _ffn(u, w1i_ref, w1o_ref)
        yield
        h = x + 0.5 * f
        h1_ref[rows, :] = h
        u = _rms(h, nm_ref[...]).astype(BF16)
        yield

        def proj(lo, hi):
            return _dot(u, win_ref[:, lo:hi])

        o = 0
        qa_ref[rows, :] = proj(o, o + QK) * (DK ** -0.5); o += QK
        ka_ref[rows, :] = proj(o, o + QK); o += QK
        va_ref[rows, :] = proj(o, o + VW).astype(BF16); o += VW
        ra = proj(o, o + VW); o += VW
        ra_ref[rows, :] = (ra * _sigmoid(ra)).astype(BF16)
        cos = cos_ref[rows, :]
        sin = sin_ref[rows, :]
        qr = proj(o, o + QK); o += QK
        qr_ref[rows, :] = qr * cos + _swap_halves(qr) * sin
        kr = proj(o, o + QK); o += QK
        kr_ref[rows, :] = (kr * cos + _swap_halves(kr) * sin) * (DK ** -0.5)
        vr_ref[rows, :] = proj(o, o + VW).astype(BF16); o += VW
        gr = proj(o, o + VW); o += VW
        gr_ref[rows, :] = (gr * _sigmoid(gr)).astype(BF16)
        yield
        a_low = proj(N_MAIN, N_MAIN + LOW_RANK)
        z = _dot(a_low.astype(BF16), wup_ref[...]) + balpha_ref[...]
        log_sig = jnp.minimum(z, 0.0) - jnp.log1p(jnp.exp(-jnp.abs(z)))
        la_ref[rows, :] = log_sig * (1.0 / GATE_TAU)
        d = ga_ref.shape[1]
        gates = _dot(u, wgate_ref[...])
        ga_ref[rows, :] = _sigmoid(gates[:, :d]).astype(BF16)
        gt_ref[rows, :] = _sigmoid(gates[:, d:]).astype(BF16)
        yield

    tile = x_ref.shape[0]
    chains = [chain(slice(lo, lo + tile // 2)) for lo in (0, tile // 2)]
    for c in _PRE_ORDER:
        next(chains[c])


_PRE_ORDER = (0, 0, 1, 0, 1, 0, 1, 0, 1, 1)


def _cast_blocks(rows, steps):
    return max(n for n in range(1, steps + 1) if rows % n == 0 and (rows // n) % BF16_ROWS == 0)


def _pre_call(x, cos, sin, weights, to_cast):
    n, d = x.shape
    tm = PRE_TILE
    steps = n // tm
    table_blocks = cos.shape[0] // tm
    row = lambda w: pl.BlockSpec((tm, w), lambda i: (i, 0))
    tab = pl.BlockSpec((tm, QK), lambda i: (i % table_blocks, 0))
    out_widths = [(d, F32), (QK, F32), (QK, F32), (VW, BF16), (QK, F32), (QK, F32), (QK, F32),
                  (VW, BF16), (VW, BF16), (VW, BF16), (d, BF16), (d, BF16)]

    def slab(w):
        blocks = _cast_blocks(w.shape[0], steps)
        return pl.BlockSpec((w.shape[0] // blocks, w.shape[1]), lambda i: (jnp.minimum(i, blocks - 1), 0))

    cast_specs = [slab(w) for w in to_cast]
    outs = pl.pallas_call(
        functools.partial(_pre_kernel, n_cast=len(to_cast)),
        grid=(steps,),
        in_specs=[row(d), tab, tab] + [_whole(w) for w in weights] + cast_specs,
        out_specs=[row(w) for w, _ in out_widths] + cast_specs,
        out_shape=[jax.ShapeDtypeStruct((n, w), t) for w, t in out_widths]
        + [jax.ShapeDtypeStruct(w.shape, BF16) for w in to_cast],
        scratch_shapes=[pltpu.VMEM((d, 2 * d), BF16)],
        compiler_params=pltpu.CompilerParams(
            dimension_semantics=("arbitrary",), vmem_limit_bytes=VMEM_LIMIT),
        name="pre",
    )(x, cos, sin, *weights, *to_cast)
    return outs[:12], outs[12:]


def _prompt_scan_kernel(qa_ref, ka_ref, la_ref, va_ref, qr_ref, kr_ref, vr_ref,
                        o_ref, sg_ref, sr_ref,
                        stg_ref, str_ref, b_ref, acc_ref, ks_ref, bs_ref, vs_ref):
    t = pl.program_id(1)
    c_len = SCAN_CHUNK
    tile = qa_ref.shape[0]
    n_chunks = tile // c_len

    @pl.when(t == 0)
    def _():
        stg_ref[...] = jnp.zeros_like(stg_ref)
        str_ref[...] = jnp.zeros_like(str_ref)

    ri = lax.broadcasted_iota(jnp.int32, (tile, tile), 0)
    ci = lax.broadcasted_iota(jnp.int32, (tile, tile), 1)
    cum = jnp.where((ci <= ri) & (ri // c_len == ci // c_len), 1.0, 0.0).astype(BF16)
    hi, mid, lo = _split3(la_ref[...])
    b_all = _dot(cum, hi) + _dot(cum, mid) + _dot(cum, lo)
    b_ref[...] = b_all
    safe = jnp.min(b_all) >= GLA_SAFE_LOG_DECAY

    r = lax.broadcasted_iota(jnp.int32, (c_len, c_len), 0)
    c = lax.broadcasted_iota(jnp.int32, (c_len, c_len), 1)
    causal = c <= r
    rowf = r.astype(F32)
    dist = (r - c).astype(F32)

    def gla_intra_by_offsets():
        seg = _head_segment_sum()
        in_chunk = lax.broadcasted_iota(jnp.int32, (tile, QK), 0) % c_len
        ks_ref[...] = ka_ref[...]
        bs_ref[...] = b_ref[...]
        vs_ref[...] = va_ref[...].astype(F32)
        acc_ref[...] = jnp.zeros_like(acc_ref)

        def body(d, carry):
            ok = in_chunk >= d
            k_sh, b_sh, v_sh = ks_ref[...], bs_ref[...], vs_ref[...]
            w = jnp.exp(jnp.where(ok, b_ref[...] - b_sh, 0.0))
            p = jnp.where(ok, qa_ref[...] * k_sh * w, 0.0).astype(BF16)
            acc_ref[...] += _dot(p, seg) * v_sh
            ks_ref[...] = pltpu.roll(k_sh, 1, 0)
            bs_ref[...] = pltpu.roll(b_sh, 1, 0)
            vs_ref[...] = pltpu.roll(v_sh, 1, 0)
            return carry

        lax.fori_loop(0, c_len, body, 0)

    def gla(factored):
        for n in range(n_chunks):
            rows = slice(n * c_len, (n + 1) * c_len)
            b = b_ref[rows, :]
            b_last = b[c_len - 1:c_len, :]
            k = ka_ref[rows, :]
            qt = (qa_ref[rows, :] * jnp.exp(b)).astype(BF16)
            kd = (k * jnp.exp(b_last - b)).astype(BF16)
            if factored:
                kt = (k * jnp.exp(-b)).astype(BF16)
            st = stg_ref[...]
            st_b = st.astype(BF16)
            stg_ref[...] = st * jnp.exp(b_last)
            for h in range(N_HEADS):
                ks = slice(h * DK, (h + 1) * DK)
                vs = slice(h * DV, (h + 1) * DV)
                v = va_ref[rows, vs]
                if factored:
                    s = jnp.where(causal, _dot_nt(qt[:, ks], kt[:, ks]), 0.0).astype(BF16)
                    intra = _dot(s, v)
                else:
                    intra = acc_ref[rows, vs]
                o_ref[rows, vs] = intra + _dot_nt(qt[:, ks], st_b[:, ks])
                stg_ref[:, ks] += _dot_tn(v, kd[:, ks])

    def retention():
        lg_lane = _head_lane_const(RET_LOG_DECAY, QK, DK)
        row_qk = lax.broadcasted_iota(jnp.int32, (c_len, QK), 0).astype(F32)
        k_decay = jnp.exp(lg_lane * (c_len - 1.0 - row_qk))
        chunk_decay = jnp.exp(lg_lane * float(c_len))
        for n in range(n_chunks):
            rows = slice(n * c_len, (n + 1) * c_len)
            q = qr_ref[rows, :].astype(BF16)
            kf = kr_ref[rows, :]
            k = kf.astype(BF16)
            kd = (kf * k_decay).astype(BF16)
            st = str_ref[...]
            st_b = st.astype(BF16)
            str_ref[...] = st * chunk_decay
            for h in range(N_HEADS):
                ks = slice(h * DK, (h + 1) * DK)
                vs = slice(VW + h * DV, VW + (h + 1) * DV)
                v = vr_ref[rows, h * DV:(h + 1) * DV]
                lg = RET_LOG_DECAY[h]
                s = (_dot_nt(q[:, ks], k[:, ks]) * jnp.where(causal, jnp.exp(lg * dist), 0.0)).astype(BF16)
                o_ref[rows, vs] = _dot(s, v) + jnp.exp(lg * (rowf + 1.0)) * _dot_nt(q[:, ks], st_b[:, ks])
                str_ref[:, ks] += _dot_tn(v, kd[:, ks])

    @pl.when(safe)
    def _():
        gla(factored=True)
        retention()

    @pl.when(jnp.logical_not(safe))
    def _():
        gla_intra_by_offsets()
        gla(factored=False)
        retention()

    @pl.when(t == pl.num_programs(1) - 1)
    def _():
        for h in range(N_HEADS):
            ks = slice(h * DK, (h + 1) * DK)
            sg_ref[0, h] = stg_ref[:, ks].T
            sr_ref[0, h] = str_ref[:, ks].T


def _prompt_scan_call(qa, ka, la, va, qr, kr, vr, batch, seq):
    tile = SCAN_TILE
    steps = seq // tile
    row = lambda w: pl.BlockSpec((tile, w), lambda b, t: (b * steps + t, 0))
    state = pl.BlockSpec((1, N_HEADS, DK, DV), lambda b, t: (b, 0, 0, 0))
    state_shape = jax.ShapeDtypeStruct((batch, N_HEADS, DK, DV), F32)
    return pl.pallas_call(
        _prompt_scan_kernel,
        grid=(batch, steps),
        in_specs=[row(QK), row(QK), row(QK), row(VW), row(QK), row(QK), row(VW)],
        out_specs=[row(2 * VW), state, state],
        out_shape=[jax.ShapeDtypeStruct((batch * seq, 2 * VW), F32), state_shape, state_shape],
        scratch_shapes=[pltpu.VMEM((DV, QK), F32), pltpu.VMEM((DV, QK), F32),
                        pltpu.VMEM((tile, QK), F32), pltpu.VMEM((tile, VW), F32),
                        pltpu.VMEM((tile, QK), F32), pltpu.VMEM((tile, QK), F32), pltpu.VMEM((tile, VW), F32)],
        compiler_params=pltpu.CompilerParams(
            dimension_semantics=("arbitrary", "arbitrary"), vmem_limit_bytes=VMEM_LIMIT),
        name="prompt_scan",
    )(qa, ka, la, va, qr, kr, vr)


def _sample_scan_kernel(qa_ref, ka_ref, la_ref, va_ref, qr_ref, kr_ref, vr_ref, sg0_ref, sr0_ref,
                        o_ref, sg_ref, sr_ref, *, steps):
    rows = qa_ref.shape[0]
    n_seq = rows // steps
    pair = 8 // steps
    t_qk = lax.broadcasted_iota(jnp.int32, (rows, QK), 0) % steps
    t_v = lax.broadcasted_iota(jnp.int32, (rows, VW), 0) % steps
    seg = _head_segment_sum()

    def down(x, d, t):
        return x if d == 0 else jnp.where(t >= d, pltpu.roll(x, d, 0), 0.0)

    g = la_ref[...]
    b = g
    for d in range(1, steps):
        b = b + down(g, d, t_qk)
    last = jnp.where(t_qk == steps - 1, b, 0.0)
    b_last = last
    for d in range(1, steps):
        b_last = b_last + pltpu.roll(last, rows - d, 0)

    qa = qa_ref[...]
    ka = ka_ref[...]
    va = va_ref[...].astype(F32)
    qr = qr_ref[...]
    kr = kr_ref[...]
    vr = vr_ref[...].astype(F32)
    lg_lane = _head_lane_const(RET_LOG_DECAY, QK, DK)
    lg_v = _head_lane_const(RET_LOG_DECAY, VW, DV)

    oa = jnp.zeros((rows, VW), F32)
    orr = jnp.zeros((rows, VW), F32)
    for d in range(steps):
        ok = t_qk >= d
        w = jnp.where(ok, jnp.exp(jnp.where(ok, b - down(b, d, t_qk), 0.0)), 0.0)
        pa = (qa * down(ka, d, t_qk) * w).astype(BF16)
        oa = oa + _dot(pa, seg) * down(va, d, t_v)
        pr = (qr * down(kr, d, t_qk) * jnp.where(ok, jnp.exp(lg_lane * float(d)), 0.0)).astype(BF16)
        orr = orr + _dot(pr, seg) * down(vr, d, t_v)

    qta = (qa * jnp.exp(b)).astype(BF16)
    kda = (ka * jnp.exp(b_last - b)).astype(BF16)
    e_hi, e_mid, e_lo = _split3(jnp.exp(b_last))
    zero = jnp.zeros_like(e_hi)
    e_parts = jnp.where(t_qk == 0, e_hi, jnp.where(t_qk == 1, e_mid, jnp.where(t_qk == 2, e_lo, zero)))
    qtr = qr.astype(BF16)
    q_scale = jnp.exp(lg_v * (t_v.astype(F32) + 1.0))
    kdr = (kr * jnp.exp(lg_lane * (steps - 1.0 - t_qk.astype(F32)))).astype(BF16)
    ones = jnp.ones((8, DV), BF16)
    slab_row = lax.broadcasted_iota(jnp.int32, (8, 1), 0) // steps
    va_b = va_ref[...]
    vr_b = vr_ref[...]

    for p in range(n_seq // pair):
        slab = slice(8 * p, 8 * p + 8)
        for h in range(N_HEADS):
            ks = slice(h * DK, (h + 1) * DK)
            vs = slice(h * DV, (h + 1) * DV)
            inter_a = jnp.zeros((8, DV), F32)
            inter_r = jnp.zeros((8, DV), F32)
            for j in range(pair):
                s_idx = p * pair + j
                mine = slab_row == j
                s0a = sg0_ref[s_idx, h]
                s0r = sr0_ref[s_idx, h]
                inter_a = jnp.where(mine, _dot(qta[slab, ks], s0a.astype(BF16)), inter_a)
                inter_r = jnp.where(mine, _dot(qtr[slab, ks], s0r.astype(BF16)), inter_r)
                e_col = _dot_tn(jnp.where(mine, e_parts[slab, ks], 0.0).astype(BF16), ones)
                kd = jnp.where(mine, kda[slab, ks], 0.0).astype(BF16)
                sg_ref[s_idx, h] = e_col * s0a + _dot_tn(kd, va_b[slab, vs])
                kd = jnp.where(mine, kdr[slab, ks], 0.0).astype(BF16)
                sr_ref[s_idx, h] = math.exp(RET_LOG_DECAY[h] * steps) * s0r + _dot_tn(kd, vr_b[slab, vs])
            o_ref[slab, vs] = oa[slab, vs] + inter_a
            o_ref[slab, VW + h * DV:VW + (h + 1) * DV] = orr[slab, vs] + q_scale[slab, vs] * inter_r


def _sample_scan_call(qa, ka, la, va, qr, kr, vr, sg0, sr0, steps):
    n_seq = sg0.shape[0]
    rows = SAMPLE_SEQS * steps
    row = lambda w: pl.BlockSpec((rows, w), lambda i: (i, 0))
    state = pl.BlockSpec((SAMPLE_SEQS, N_HEADS, DK, DV), lambda i: (i, 0, 0, 0))
    state_shape = jax.ShapeDtypeStruct(sg0.shape, F32)
    return pl.pallas_call(
        functools.partial(_sample_scan_kernel, steps=steps),
        grid=(n_seq // SAMPLE_SEQS,),
        in_specs=[row(QK), row(QK), row(QK), row(VW), row(QK), row(QK), row(VW), state, state],
        out_specs=[row(2 * VW), state, state],
        out_shape=[jax.ShapeDtypeStruct((n_seq * steps, 2 * VW), F32), state_shape, state_shape],
        compiler_params=pltpu.CompilerParams(
            dimension_semantics=("arbitrary",), vmem_limit_bytes=VMEM_LIMIT),
        name="sample_scan",
    )(qa, ka, la, va, qr, kr, vr, sg0, sr0)


def _post_chain(rows, o_ref, h1_ref, ra_ref, gr_ref, ga_ref, gt_ref, p_ref,
                gn_ref, wout_ref, n2_ref, w2i_ref, w2o_ref, npl_ref, wpg_ref, wpp_ref, nf_ref, y_ref, final):
    o = o_ref[rows, :]
    normed = []
    for h in range(2 * N_HEADS):
        oh = o[:, h * DV:(h + 1) * DV]
        normed.append(oh * lax.rsqrt(jnp.mean(oh * oh, axis=-1, keepdims=True) + EPS))
    on = jnp.concatenate(normed, axis=-1) * gn_ref[...]
    oa = (on[:, :VW] * ra_ref[rows, :].astype(F32)).astype(BF16)
    orr = (on[:, VW:] * gr_ref[rows, :].astype(F32)).astype(BF16)
    yield
    mix = (ga_ref[rows, :].astype(F32) * _dot(oa, wout_ref[:VW, :])
           + gt_ref[rows, :].astype(F32) * _dot(orr, wout_ref[VW:, :]))
    yield
    h = h1_ref[rows, :] + mix
    u = _rms(h, n2_ref[...]).astype(BF16)
    yield
    f = _ffn(u, w2i_ref, w2o_ref)
    yield
    h = h + 0.5 * f
    u = _rms(h, npl_ref[...]).astype(BF16)
    yield
    gate = _dot(u, wpg_ref[...])
    proj = _dot(p_ref[rows, :].astype(BF16), wpp_ref[...])
    yield
    h = h + proj * _sigmoid(gate)
    y_ref[rows, :] = _rms(h, nf_ref[...]) if final else h
    yield


_POST_ORDER = (0, 0, 1, 0, 1, 0, 1, 1, 0, 0, 1, 0, 1, 1)


def _post_kernel(*refs, final):
    tile = refs[0].shape[0]
    chains = [_post_chain(slice(lo, lo + tile // 2), *refs, final) for lo in (0, tile // 2)]
    for c in _POST_ORDER:
        next(chains[c])


def _post_call(o, h1, ra, gr, ga, gt, p, weights, final):
    n, d = h1.shape
    tm = POST_TILE
    row = lambda w: pl.BlockSpec((tm, w), lambda i: (i, 0))
    return pl.pallas_call(
        functools.partial(_post_kernel, final=final),
        grid=(n // tm,),
        in_specs=[row(2 * VW), row(d), row(VW), row(VW), row(d), row(d), row(p.shape[1])]
        + [_whole(w) for w in weights],
        out_specs=row(d),
        out_shape=jax.ShapeDtypeStruct((n, d), F32),
        compiler_params=pltpu.CompilerParams(
            dimension_semantics=("arbitrary",), vmem_limit_bytes=VMEM_LIMIT),
        name="post",
    )(o, h1, ra, gr, ga, gt, p, *weights)


def _rotary_tables(pos, rows):
    half = DK // 2
    freq = ROPE_BASE ** (-np.arange(half, dtype=np.float64) / half)
    ang = np.asarray(pos, np.float64)[:, None] * freq[None, :]
    cos, sin = np.cos(ang), np.sin(ang)
    reps = (rows // len(pos), N_HEADS)
    return (jnp.asarray(np.tile(np.concatenate([cos, cos], axis=-1), reps), F32),
            jnp.asarray(np.tile(np.concatenate([-sin, sin], axis=-1), reps), F32))


def kernel(x_prompt, x_sample, state_gla, state_ret, p_prompt, p_sample, norm_ffn1, w_ffn1_in, w_ffn1_out, norm_mix, w_in, w_alpha_up, b_alpha, gn_gla, gn_ret, w_out, norm_ffn2, w_ffn2_in, w_ffn2_out, norm_ple, w_ple_gate, w_ple_proj, norm_final):
    bp, tp, d = x_prompt.shape
    bs, ts, _ = x_sample.shape
    depth = w_in.shape[0]
    assert w_in.shape[2] == N_MAIN + LOW_RANK + 2 * d
    assert tp % SCAN_TILE == 0 and tp % REF_CHUNK == 0 and 3 <= ts < REF_CHUNK and 8 % ts == 0
    assert (bs * ts) % POST_TILE == 0 and (bs * ts) % PRE_TILE == 0 and bs % SAMPLE_SEQS == 0

    cos_p, sin_p = _rotary_tables(np.arange(tp), tp)
    cos_s, sin_s = _rotary_tables(PAST_LEN + np.arange(ts), PRE_TILE)

    hp = x_prompt.reshape(bp * tp, d)
    hs = x_sample.reshape(bs * ts, d)
    row = lambda v: v.reshape(1, -1).astype(F32)
    gla_p, ret_p, gla_s, ret_s = [], [], [], []
    for i in range(depth):
        pre_w = (row(norm_ffn1[i]), w_ffn1_in[i].astype(BF16), w_ffn1_out[i].astype(BF16), row(norm_mix[i]),
                 w_in[i].astype(BF16), w_alpha_up[i].astype(BF16), row(b_alpha[i]))
        post_cast = (w_out[i], w_ffn2_in[i], w_ffn2_out[i], w_ple_gate[i], w_ple_proj[i])
        final = i == depth - 1

        (h1, qa, ka, va, la, qr, kr, vr, ra, gr, ga, gt), post_bf = _pre_call(hp, cos_p, sin_p, pre_w, post_cast)
        wout_b, w2i_b, w2o_b, wpg_b, wpp_b = post_bf
        post_w = (row(jnp.concatenate([gn_gla[i], gn_ret[i]])), wout_b, row(norm_ffn2[i]), w2i_b, w2o_b,
                  row(norm_ple[i]), wpg_b, wpp_b, row(norm_final))
        o, sg, sr = _prompt_scan_call(qa, ka, la, va, qr, kr, vr, bp, tp)
        hp = _post_call(o, h1, ra, gr, ga, gt, p_prompt[i].reshape(bp * tp, -1), post_w, final)
        gla_p.append(sg.astype(state_gla.dtype))
        ret_p.append(sr.astype(state_ret.dtype))

        (h1, qa, ka, va, la, qr, kr, vr, ra, gr, ga, gt), _ = _pre_call(hs, cos_s, sin_s, pre_w, ())
        o, sg, sr = _sample_scan_call(qa, ka, la, va, qr, kr, vr,
                                      state_gla[i].astype(F32), state_ret[i].astype(F32), ts)
        hs = _post_call(o, h1, ra, gr, ga, gt, p_sample[i].reshape(bs * ts, -1), post_w, final)
        gla_s.append(sg.astype(state_gla.dtype))
        ret_s.append(sr.astype(state_ret.dtype))

    return (hp.reshape(bp, tp, d), hs.reshape(bs, ts, d),
            jnp.stack(gla_p), jnp.stack(ret_p), jnp.stack(gla_s), jnp.stack(ret_s))
```

```python
import functools
import math

import numpy as np

import jax
import jax.numpy as jnp
from jax import lax
from jax.experimental import pallas as pl
from jax.experimental.pallas import tpu as pltpu

F32 = jnp.float32
BF16 = jnp.bfloat16

N_HEADS = 4
DK = 64
DV = 128
QK = N_HEADS * DK
VW = N_HEADS * DV
N_MAIN = 4 * QK + 4 * VW
LOW_RANK = 16
GATE_TAU = 16.0
ROPE_BASE = 10000.0
PAST_LEN = 16384
REF_CHUNK = 64
EPS = 1e-6
RET_LOG_DECAY = tuple(math.log1p(-(2.0 ** (-5.0 - h))) for h in range(N_HEADS))

PRE_TILE = 256
POST_TILE = 512
SCAN_CHUNK = 128
SAMPLE_SEQS = 8
BF16_ROWS = 16
MXU_TILE = 256
VMEM_LIMIT = 56 * 1024 * 1024
GLA_SAFE_LOG_DECAY = -60.0

NT = (((1,), (1,)), ((), ()))
TN = (((0,), (0,)), ((), ()))


def _dot(a, b):
    return jnp.dot(a, b, preferred_element_type=F32)


def _dot_nt(a, b):
    return lax.dot_general(a, b, NT, preferred_element_type=F32)


def _dot_tn(a, b):
    return lax.dot_general(a, b, TN, preferred_element_type=F32)


def _rms(x, w):
    return x * lax.rsqrt(jnp.mean(x * x, axis=-1, keepdims=True) + EPS) * w


def _sigmoid(x):
    return jax.nn.sigmoid(x)


def _ffn(u, wi_ref, wo_ref):
    d_ff = wo_ref.shape[0]
    cut = -(-d_ff // (2 * MXU_TILE)) * MXU_TILE
    out = None
    for lo, hi in ((0, cut), (cut, d_ff)):
        a = _dot(u, wi_ref[:, lo:hi])
        b = _dot(u, wi_ref[:, d_ff + lo:d_ff + hi])
        g = (a * _sigmoid(a) * b).astype(BF16)
        part = _dot(g, wo_ref[lo:hi, :])
        out = part if out is None else out + part
    return out


def _split3(x):
    hi = x.astype(BF16)
    r1 = x - hi.astype(F32)
    mid = r1.astype(BF16)
    lo = (r1 - mid.astype(F32)).astype(BF16)
    return hi, mid, lo


def _head_lane_const(values, width, per_head):
    lane = lax.broadcasted_iota(jnp.int32, (1, width), 1) // per_head
    out = jnp.full((1, width), values[-1], F32)
    for h in range(len(values) - 2, -1, -1):
        out = jnp.where(lane == h, values[h], out)
    return out


def _head_segment_sum():
    lane_head = lax.broadcasted_iota(jnp.int32, (QK, VW), 0) // DK
    col_head = lax.broadcasted_iota(jnp.int32, (QK, VW), 1) // DV
    return jnp.where(lane_head == col_head, 1.0, 0.0).astype(BF16)


def _swap_halves(x):
    n = x.shape[-1]
    lane = lax.broadcasted_iota(jnp.int32, x.shape, 1)
    fwd = pltpu.roll(x, n - DK // 2, 1)
    bwd = pltpu.roll(x, DK // 2, 1)
    return jnp.where(lane % DK < DK // 2, fwd, bwd)


def _whole(_):
    return pl.BlockSpec(memory_space=pltpu.VMEM)


def _pre_chain(rows, x_ref, cos_ref, sin_ref, n1_ref, w1i_ref, w1o_ref, nm_ref, win_ref, wup_ref, balpha_ref,
               wgate_ref, h1_ref, ra_ref, gr_ref, ga_ref, gt_ref,
               qa_dst, ka_dst, va_dst, la_dst, qr_dst, kr_dst, vr_dst, decay_sums):
    x = x_ref[rows, :]
    u = _rms(x, n1_ref[...]).astype(BF16)
    yield
    f = _ffn(u, w1i_ref, w1o_ref)
    yield
    h = x + 0.5 * f
    h1_ref[rows, :] = h
    u = _rms(h, nm_ref[...]).astype(BF16)
    yield

    main = _dot(u, win_ref[:, :N_MAIN])

    def proj(lo, hi):
        return main[:, lo:hi]

    o = 0
    qa_dst[rows, :] = proj(o, o + QK) * (DK ** -0.5); o += QK
    ka_dst[rows, :] = proj(o, o + QK); o += QK
    va_dst[rows, :] = proj(o, o + VW).astype(BF16); o += VW
    ra = proj(o, o + VW); o += VW
    ra_ref[rows, :] = (ra * _sigmoid(ra)).astype(BF16)
    cos = cos_ref[rows, :]
    sin = sin_ref[rows, :]
    qr = proj(o, o + QK); o += QK
    qr_dst[rows, :] = qr * cos + _swap_halves(qr) * sin
    kr = proj(o, o + QK); o += QK
    kr_dst[rows, :] = (kr * cos + _swap_halves(kr) * sin) * (DK ** -0.5)
    vr_dst[rows, :] = proj(o, o + VW).astype(BF16); o += VW
    gr = proj(o, o + VW); o += VW
    gr_ref[rows, :] = (gr * _sigmoid(gr)).astype(BF16)
    yield
    a_low = _dot(u, win_ref[:, N_MAIN:N_MAIN + LOW_RANK])
    z = _dot(a_low.astype(BF16), wup_ref[...]) + balpha_ref[...]
    log_sig = jnp.minimum(z, 0.0) - jnp.log1p(jnp.exp(-jnp.abs(z)))
    la = log_sig * (1.0 / GATE_TAU)
    la_dst[rows, :] = la
    decay_sums.append(jnp.min(jnp.sum(la, axis=0, keepdims=True)))
    d = ga_ref.shape[1]
    gates = _dot(u, wgate_ref[...])
    ga_ref[rows, :] = _sigmoid(gates[:, :d]).astype(BF16)
    gt_ref[rows, :] = _sigmoid(gates[:, d:]).astype(BF16)
    yield


_PRE_ORDER = (0, 0, 1, 0, 1, 0, 1, 0, 1, 1)


def _pre_kernel(x_ref, cos_ref, sin_ref, n1_ref, w1i_ref, w1o_ref, nm_ref, win_ref, wup_ref, balpha_ref,
                h1_ref, qa_ref, ka_ref, va_ref, la_ref, qr_ref, kr_ref, vr_ref, ra_ref, gr_ref, ga_ref, gt_ref,
                wgate_ref):
    @pl.when(pl.program_id(0) == 0)
    def _():
        wgate_ref[...] = win_ref[:, N_MAIN + LOW_RANK:]

    tile = x_ref.shape[0]
    chains = [_pre_chain(slice(lo, lo + tile // 2), x_ref, cos_ref, sin_ref, n1_ref, w1i_ref, w1o_ref, nm_ref,
                         win_ref, wup_ref, balpha_ref, wgate_ref, h1_ref, ra_ref, gr_ref, ga_ref, gt_ref,
                         qa_ref, ka_ref, va_ref, la_ref, qr_ref, kr_ref, vr_ref, [])
              for lo in (0, tile // 2)]
    for c in _PRE_ORDER:
        next(chains[c])


def _pre_call(x, cos, sin, weights):
    n, d = x.shape
    tm = PRE_TILE
    table_blocks = cos.shape[0] // tm
    row = lambda w: pl.BlockSpec((tm, w), lambda i: (i, 0))
    tab = pl.BlockSpec((tm, QK), lambda i: (i % table_blocks, 0))
    out_widths = [(d, F32), (QK, F32), (QK, F32), (VW, BF16), (QK, F32), (QK, F32), (QK, F32),
                  (VW, BF16), (VW, BF16), (VW, BF16), (d, BF16), (d, BF16)]
    return pl.pallas_call(
        _pre_kernel,
        grid=(n // tm,),
        in_specs=[row(d), tab, tab] + [_whole(w) for w in weights],
        out_specs=[row(w) for w, _ in out_widths],
        out_shape=[jax.ShapeDtypeStruct((n, w), t) for w, t in out_widths],
        scratch_shapes=[pltpu.VMEM((d, 2 * d), BF16)],
        compiler_params=pltpu.CompilerParams(
            dimension_semantics=("arbitrary",), vmem_limit_bytes=VMEM_LIMIT),
        name="pre",
    )(x, cos, sin, *weights)


def _front_kernel(x_ref, cos_ref, sin_ref, n1_ref, w1i_ref, w1o_ref, nm_ref, win_ref, wup_ref, balpha_ref,
                  *rest, n_cast, tiles_per_seq):
    cast_in = rest[:n_cast]
    h1_ref, ra_ref, gr_ref, ga_ref, gt_ref, o_ref, sg_ref, sr_ref = rest[n_cast:n_cast + 8]
    cast_out = rest[n_cast + 8:2 * n_cast + 8]
    wgate_ref = rest[2 * n_cast + 8]
    handoff = rest[2 * n_cast + 9:2 * n_cast + 16]
    stg_ref, str_ref, b_ref, acc_ref, ks_ref, bs_ref, vs_ref, decay_ref = rest[2 * n_cast + 16:]
    i = pl.program_id(0)
    slot = i % 2
    qa_w, ka_w, va_w, la_w, qr_w, kr_w, vr_w = (ref.at[slot] for ref in handoff)
    qa_s, ka_s, va_s, la_s, qr_s, kr_s, vr_s = (ref.at[1 - slot] for ref in handoff)
    c_len = SCAN_CHUNK
    tile = x_ref.shape[0]
    n_chunks = tile // c_len
    r = lax.broadcasted_iota(jnp.int32, (c_len, c_len), 0)
    c = lax.broadcasted_iota(jnp.int32, (c_len, c_len), 1)
    causal = c <= r

    def chunk_cumsum(la):
        tril = jnp.where(causal, 1.0, 0.0).astype(BF16)
        hi, mid, lo = _split3(la)
        return _dot(tril, hi) + _dot(tril, mid) + _dot(tril, lo)

    @pl.when(i == 0)
    def _():
        wgate_ref[...] = win_ref[:, N_MAIN + LOW_RANK:]
        for ref in (*handoff, acc_ref):
            ref[...] = jnp.zeros_like(ref)
        decay_ref[0] = 0.0

    @pl.when((i == 0) | (i % tiles_per_seq == 1))
    def _():
        stg_ref[...] = jnp.zeros_like(stg_ref)
        str_ref[...] = jnp.zeros_like(str_ref)

    @pl.when(decay_ref[0] < GLA_SAFE_LOG_DECAY)
    def _():
        seg = _head_segment_sum()
        in_chunk = lax.broadcasted_iota(jnp.int32, (tile, QK), 0) % c_len
        for n in range(n_chunks):
            rows = slice(n * c_len, (n + 1) * c_len)
            b_ref[rows, :] = chunk_cumsum(la_s[rows, :])
        ks_ref[...] = ka_s[...]
        bs_ref[...] = b_ref[...]
        vs_ref[...] = va_s[...].astype(F32)
        acc_ref[...] = jnp.zeros_like(acc_ref)

        def body(d, carry):
            ok = in_chunk >= d
            k_sh, b_sh, v_sh = ks_ref[...], bs_ref[...], vs_ref[...]
            w = jnp.exp(jnp.where(ok, b_ref[...] - b_sh, 0.0))
            p = jnp.where(ok, qa_s[...] * k_sh * w, 0.0).astype(BF16)
            acc_ref[...] += _dot(p, seg) * v_sh
            ks_ref[...] = pltpu.roll(k_sh, 1, 0)
            bs_ref[...] = pltpu.roll(b_sh, 1, 0)
            vs_ref[...] = pltpu.roll(v_sh, 1, 0)
            return carry

        lax.fori_loop(0, c_len, body, 0)

    for src, dst in zip(cast_in, cast_out):
        dst[...] = src[...].astype(BF16)

    def scan():
        exact = decay_ref[0] < GLA_SAFE_LOG_DECAY
        dist = (r - c).astype(F32)
        lg_lane = _head_lane_const(RET_LOG_DECAY, QK, DK)
        row_qk = lax.broadcasted_iota(jnp.int32, (c_len, QK), 0).astype(F32)
        k_decay = jnp.exp(lg_lane * (c_len - 1.0 - row_qk))
        q_decay = jnp.exp(lg_lane * (row_qk + 1.0))
        chunk_decay = jnp.exp(lg_lane * float(c_len))
        gla, ret = [], []
        for n in range(n_chunks):
            rows = slice(n * c_len, (n + 1) * c_len)
            b = chunk_cumsum(la_s[rows, :])
            b_last = b[c_len - 1:c_len, :]
            k = ka_s[rows, :]
            qt = (qa_s[rows, :] * jnp.exp(b)).astype(BF16)
            kt = (k * jnp.exp(jnp.minimum(-b, -GLA_SAFE_LOG_DECAY))).astype(BF16)
            kd = (k * jnp.exp(b_last - b)).astype(BF16)
            gla.append((qt, kt, kd, jnp.exp(b_last), va_s[rows, :]))
            kf = kr_s[rows, :]
            qf = qr_s[rows, :]
            ret.append((qf.astype(BF16), (qf * q_decay).astype(BF16), kf.astype(BF16),
                        (kf * k_decay).astype(BF16), vr_s[rows, :]))
        yield
        for n in range(n_chunks):
            rows = slice(n * c_len, (n + 1) * c_len)
            qt, kt, kd, e_last, v_all = gla[n]
            st = stg_ref[...]
            st_b = st.astype(BF16)
            stg_ref[...] = st * e_last
            for h in range(N_HEADS):
                ks = slice(h * DK, (h + 1) * DK)
                vs = slice(h * DV, (h + 1) * DV)
                v = v_all[:, vs]
                s = jnp.where(causal, _dot_nt(qt[:, ks], kt[:, ks]), 0.0).astype(BF16)
                intra = jnp.where(exact, acc_ref[rows, vs], _dot(s, v))
                o_ref[rows, vs] = intra + _dot_nt(qt[:, ks], st_b[:, ks])
                stg_ref[:, ks] += _dot_tn(v, kd[:, ks])
            yield
        for n in range(n_chunks):
            rows = slice(n * c_len, (n + 1) * c_len)
            q, qd, k, kd, v_all = ret[n]
            st = str_ref[...]
            st_b = st.astype(BF16)
            str_ref[...] = st * chunk_decay
            for h in range(N_HEADS):
                ks = slice(h * DK, (h + 1) * DK)
                vs = slice(h * DV, (h + 1) * DV)
                v = v_all[:, vs]
                lg = RET_LOG_DECAY[h]
                s = (_dot_nt(q[:, ks], k[:, ks]) * jnp.where(causal, jnp.exp(lg * dist), 0.0)).astype(BF16)
                o_ref[rows, VW + h * DV:VW + (h + 1) * DV] = _dot(s, v) + _dot_nt(qd[:, ks], st_b[:, ks])
                str_ref[:, ks] += _dot_tn(v, kd[:, ks])
            yield

    decay_sums = []
    chains = [_pre_chain(slice(lo, lo + tile // 2), x_ref, cos_ref, sin_ref, n1_ref, w1i_ref, w1o_ref, nm_ref,
                         win_ref, wup_ref, balpha_ref, wgate_ref, h1_ref, ra_ref, gr_ref, ga_ref, gt_ref,
                         qa_w, ka_w, va_w, la_w, qr_w, kr_w, vr_w, decay_sums)
              for lo in (0, tile // 2)]
    chains.append(scan())
    for ch in _FRONT_ORDER:
        next(chains[ch])
    decay_ref[0] = jnp.minimum(decay_sums[0], decay_sums[1])

    @pl.when((i > 0) & (i % tiles_per_seq == 0))
    def _():
        for h in range(N_HEADS):
            ks = slice(h * DK, (h + 1) * DK)
            sg_ref[0, h] = stg_ref[:, ks].T
            sr_ref[0, h] = str_ref[:, ks].T


_FRONT_ORDER = (2, 0, 0, 1, 2, 0, 1, 2, 0, 1, 2, 0, 1, 2, 1)


def _cast_blocks(rows, steps):
    return max(n for n in range(1, steps + 1) if rows % n == 0 and (rows // n) % BF16_ROWS == 0)


def _front_call(x, cos, sin, weights, to_cast, batch, seq):
    n, d = x.shape
    tm = PRE_TILE
    assert tm == 2 * SCAN_CHUNK and seq % tm == 0
    n_tiles = n // tm
    tiles_per_seq = seq // tm
    table_blocks = cos.shape[0] // tm
    cur = lambda i: jnp.minimum(i, n_tiles - 1)
    prev = lambda i: jnp.maximum(i - 1, 0)
    row = lambda w: pl.BlockSpec((tm, w), lambda i: (cur(i), 0))
    tab = pl.BlockSpec((tm, QK), lambda i: (cur(i) % table_blocks, 0))
    state = pl.BlockSpec((1, N_HEADS, DK, DV), lambda i: (prev(i) // tiles_per_seq, 0, 0, 0))
    state_shape = jax.ShapeDtypeStruct((batch, N_HEADS, DK, DV), F32)
    out_widths = [(d, F32), (VW, BF16), (VW, BF16), (d, BF16), (d, BF16)]

    def slab(w):
        blocks = _cast_blocks(w.shape[0], n_tiles)
        return pl.BlockSpec((w.shape[0] // blocks, w.shape[1]), lambda i: (jnp.minimum(i, blocks - 1), 0))

    cast_specs = [slab(w) for w in to_cast]
    vm = lambda rows, w, t: pltpu.VMEM((rows, w), t)
    two = lambda w, t: pltpu.VMEM((2, tm, w), t)
    outs = pl.pallas_call(
        functools.partial(_front_kernel, n_cast=len(to_cast), tiles_per_seq=tiles_per_seq),
        grid=(n_tiles + 1,),
        in_specs=[row(d), tab, tab] + [_whole(w) for w in weights] + cast_specs,
        out_specs=[row(w) for w, _ in out_widths]
        + [pl.BlockSpec((tm, 2 * VW), lambda i: (prev(i), 0)), state, state] + cast_specs,
        out_shape=[jax.ShapeDtypeStruct((n, w), t) for w, t in out_widths]
        + [jax.ShapeDtypeStruct((n, 2 * VW), F32), state_shape, state_shape]
        + [jax.ShapeDtypeStruct(w.shape, BF16) for w in to_cast],
        scratch_shapes=[vm(d, 2 * d, BF16),
                        two(QK, F32), two(QK, F32), two(VW, BF16), two(QK, F32),
                        two(QK, F32), two(QK, F32), two(VW, BF16),
                        vm(DV, QK, F32), vm(DV, QK, F32),
                        vm(tm, QK, F32), vm(tm, VW, F32), vm(tm, QK, F32), vm(tm, QK, F32), vm(tm, VW, F32),
                        pltpu.SMEM((1,), F32)],
        compiler_params=pltpu.CompilerParams(
            dimension_semantics=("arbitrary",), vmem_limit_bytes=VMEM_LIMIT),
        name="front",
    )(x, cos, sin, *weights, *to_cast)
    return outs[:8], outs[8:]


def _sample_scan_kernel(qa_ref, ka_ref, la_ref, va_ref, qr_ref, kr_ref, vr_ref, sg0_ref, sr0_ref,
                        o_ref, sg_ref, sr_ref, *, steps):
    rows = qa_ref.shape[0]
    n_seq = rows // steps
    pair = 8 // steps
    t_qk = lax.broadcasted_iota(jnp.int32, (rows, QK), 0) % steps
    t_v = lax.broadcasted_iota(jnp.int32, (rows, VW), 0) % steps
    seg = _head_segment_sum()

    def down(x, d, t):
        return x if d == 0 else jnp.where(t >= d, pltpu.roll(x, d, 0), 0.0)

    g = la_ref[...]
    b = g
    for d in range(1, steps):
        b = b + down(g, d, t_qk)
    last = jnp.where(t_qk == steps - 1, b, 0.0)
    b_last = last
    for d in range(1, steps):
        b_last = b_last + pltpu.roll(last, rows - d, 0)

    qa = qa_ref[...]
    ka = ka_ref[...]
    va = va_ref[...].astype(F32)
    qr = qr_ref[...]
    kr = kr_ref[...]
    vr = vr_ref[...].astype(F32)
    lg_lane = _head_lane_const(RET_LOG_DECAY, QK, DK)
    lg_v = _head_lane_const(RET_LOG_DECAY, VW, DV)

    oa = jnp.zeros((rows, VW), F32)
    orr = jnp.zeros((rows, VW), F32)
    for d in range(steps):
        ok = t_qk >= d
        w = jnp.where(ok, jnp.exp(jnp.where(ok, b - down(b, d, t_qk), 0.0)), 0.0)
        pa = (qa * down(ka, d, t_qk) * w).astype(BF16)
        oa = oa + _dot(pa, seg) * down(va, d, t_v)
        pr = (qr * down(kr, d, t_qk) * jnp.where(ok, jnp.exp(lg_lane * float(d)), 0.0)).astype(BF16)
        orr = orr + _dot(pr, seg) * down(vr, d, t_v)

    qta = (qa * jnp.exp(b)).astype(BF16)
    kda = (ka * jnp.exp(b_last - b)).astype(BF16)
    e_hi, e_mid, e_lo = _split3(jnp.exp(b_last))
    zero = jnp.zeros_like(e_hi)
    e_parts = jnp.where(t_qk == 0, e_hi, jnp.where(t_qk == 1, e_mid, jnp.where(t_qk == 2, e_lo, zero)))
    qtr = qr.astype(BF16)
    q_scale = jnp.exp(lg_v * (t_v.astype(F32) + 1.0))
    kdr = (kr * jnp.exp(lg_lane * (steps - 1.0 - t_qk.astype(F32)))).astype(BF16)
    ones = jnp.ones((8, DV), BF16)
    slab_row = lax.broadcasted_iota(jnp.int32, (8, 1), 0) // steps
    va_b = va_ref[...]
    vr_b = vr_ref[...]

    for p in range(n_seq // pair):
        slab = slice(8 * p, 8 * p + 8)
        for h in range(N_HEADS):
            ks = slice(h * DK, (h + 1) * DK)
            vs = slice(h * DV, (h + 1) * DV)
            inter_a = jnp.zeros((8, DV), F32)
            inter_r = jnp.zeros((8, DV), F32)
            for j in range(pair):
                s_idx = p * pair + j
                mine = slab_row == j
                s0a = sg0_ref[s_idx, h]
                s0r = sr0_ref[s_idx, h]
                inter_a = jnp.where(mine, _dot(qta[slab, ks], s0a.astype(BF16)), inter_a)
                inter_r = jnp.where(mine, _dot(qtr[slab, ks], s0r.astype(BF16)), inter_r)
                e_col = _dot_tn(jnp.where(mine, e_parts[slab, ks], 0.0).astype(BF16), ones)
                kd = jnp.where(mine, kda[slab, ks], 0.0).astype(BF16)
                sg_ref[s_idx, h] = e_col * s0a + _dot_tn(kd, va_b[slab, vs])
                kd = jnp.where(mine, kdr[slab, ks], 0.0).astype(BF16)
                sr_ref[s_idx, h] = math.exp(RET_LOG_DECAY[h] * steps) * s0r + _dot_tn(kd, vr_b[slab, vs])
            o_ref[slab, vs] = oa[slab, vs] + inter_a
            o_ref[slab, VW + h * DV:VW + (h + 1) * DV] = orr[slab, vs] + q_scale[slab, vs] * inter_r


def _sample_scan_call(qa, ka, la, va, qr, kr, vr, sg0, sr0, steps):
    n_seq = sg0.shape[0]
    rows = SAMPLE_SEQS * steps
    row = lambda w: pl.BlockSpec((rows, w), lambda i: (i, 0))
    state = pl.BlockSpec((SAMPLE_SEQS, N_HEADS, DK, DV), lambda i: (i, 0, 0, 0))
    state_shape = jax.ShapeDtypeStruct(sg0.shape, F32)
    return pl.pallas_call(
        functools.partial(_sample_scan_kernel, steps=steps),
        grid=(n_seq // SAMPLE_SEQS,),
        in_specs=[row(QK), row(QK), row(QK), row(VW), row(QK), row(QK), row(VW), state, state],
        out_specs=[row(2 * VW), state, state],
        out_shape=[jax.ShapeDtypeStruct((n_seq * steps, 2 * VW), F32), state_shape, state_shape],
        compiler_params=pltpu.CompilerParams(
            dimension_semantics=("arbitrary",), vmem_limit_bytes=VMEM_LIMIT),
        name="sample_scan",
    )(qa, ka, la, va, qr, kr, vr, sg0, sr0)


def _post_chain(rows, o_ref, h1_ref, ra_ref, gr_ref, ga_ref, gt_ref, p_ref,
                gn_ref, wout_ref, n2_ref, w2i_ref, w2o_ref, npl_ref, wpg_ref, wpp_ref, nf_ref, y_ref, final):
    o = o_ref[rows, :]
    normed = []
    for h in range(2 * N_HEADS):
        oh = o[:, h * DV:(h + 1) * DV]
        normed.append(oh * lax.rsqrt(jnp.mean(oh * oh, axis=-1, keepdims=True) + EPS))
    on = jnp.concatenate(normed, axis=-1) * gn_ref[...]
    oa = (on[:, :VW] * ra_ref[rows, :].astype(F32)).astype(BF16)
    orr = (on[:, VW:] * gr_ref[rows, :].astype(F32)).astype(BF16)
    yield
    mix = (ga_ref[rows, :].astype(F32) * _dot(oa, wout_ref[:VW, :])
           + gt_ref[rows, :].astype(F32) * _dot(orr, wout_ref[VW:, :]))
    yield
    h = h1_ref[rows, :] + mix
    u = _rms(h, n2_ref[...]).astype(BF16)
    yield
    f = _ffn(u, w2i_ref, w2o_ref)
    yield
    h = h + 0.5 * f
    u = _rms(h, npl_ref[...]).astype(BF16)
    yield
    gate = _dot(u, wpg_ref[...])
    proj = _dot(p_ref[rows, :].astype(BF16), wpp_ref[...])
    yield
    h = h + proj * _sigmoid(gate)
    y_ref[rows, :] = _rms(h, nf_ref[...]) if final else h
    yield


_POST_ORDER = (0, 0, 1, 0, 1, 0, 1, 1, 0, 0, 1, 0, 1, 1)


def _post_kernel(*refs, final):
    tile = refs[0].shape[0]
    chains = [_post_chain(slice(lo, lo + tile // 2), *refs, final) for lo in (0, tile // 2)]
    for c in _POST_ORDER:
        next(chains[c])


def _post_call(o, h1, ra, gr, ga, gt, p, weights, final):
    n, d = h1.shape
    tm = POST_TILE
    row = lambda w: pl.BlockSpec((tm, w), lambda i: (i, 0))
    return pl.pallas_call(
        functools.partial(_post_kernel, final=final),
        grid=(n // tm,),
        in_specs=[row(2 * VW), row(d), row(VW), row(VW), row(d), row(d), row(p.shape[1])]
        + [_whole(w) for w in weights],
        out_specs=row(d),
        out_shape=jax.ShapeDtypeStruct((n, d), F32),
        compiler_params=pltpu.CompilerParams(
            dimension_semantics=("arbitrary",), vmem_limit_bytes=VMEM_LIMIT),
        name="post",
    )(o, h1, ra, gr, ga, gt, p, *weights)


def _rotary_tables(pos, rows):
    half = DK // 2
    freq = ROPE_BASE ** (-np.arange(half, dtype=np.float64) / half)
    ang = np.asarray(pos, np.float64)[:, None] * freq[None, :]
    cos, sin = np.cos(ang), np.sin(ang)
    reps = (rows // len(pos), N_HEADS)
    return (jnp.asarray(np.tile(np.concatenate([cos, cos], axis=-1), reps), F32),
            jnp.asarray(np.tile(np.concatenate([-sin, sin], axis=-1), reps), F32))


def kernel(x_prompt, x_sample, state_gla, state_ret, p_prompt, p_sample, norm_ffn1, w_ffn1_in, w_ffn1_out, norm_mix, w_in, w_alpha_up, b_alpha, gn_gla, gn_ret, w_out, norm_ffn2, w_ffn2_in, w_ffn2_out, norm_ple, w_ple_gate, w_ple_proj, norm_final):
    bp, tp, d = x_prompt.shape
    bs, ts, _ = x_sample.shape
    depth = w_in.shape[0]
    assert w_in.shape[2] == N_MAIN + LOW_RANK + 2 * d
    assert tp % PRE_TILE == 0 and tp % POST_TILE == 0 and tp % REF_CHUNK == 0 and 3 <= ts < REF_CHUNK and 8 % ts == 0
    assert (bs * ts) % POST_TILE == 0 and (bs * ts) % PRE_TILE == 0 and bs % SAMPLE_SEQS == 0

    cos_p, sin_p = _rotary_tables(np.arange(tp), tp)
    cos_s, sin_s = _rotary_tables(PAST_LEN + np.arange(ts), PRE_TILE)

    hp = x_prompt.reshape(bp * tp, d)
    hs = x_sample.reshape(bs * ts, d)
    row = lambda v: v.reshape(1, -1).astype(F32)
    gla_p, ret_p, gla_s, ret_s = [], [], [], []
    for i in range(depth):
        pre_w = (row(norm_ffn1[i]), w_ffn1_in[i].astype(BF16), w_ffn1_out[i].astype(BF16), row(norm_mix[i]),
                 w_in[i].astype(BF16), w_alpha_up[i].astype(BF16), row(b_alpha[i]))
        post_cast = (w_out[i], w_ffn2_in[i], w_ffn2_out[i], w_ple_gate[i], w_ple_proj[i])
        final = i == depth - 1

        (h1, ra, gr, ga, gt, o, sg, sr), post_bf = _front_call(hp, cos_p, sin_p, pre_w, post_cast, bp, tp)
        wout_b, w2i_b, w2o_b, wpg_b, wpp_b = post_bf
        post_w = (row(jnp.concatenate([gn_gla[i], gn_ret[i]])), wout_b, row(norm_ffn2[i]), w2i_b, w2o_b,
                  row(norm_ple[i]), wpg_b, wpp_b, row(norm_final))
        hp = _post_call(o, h1, ra, gr, ga, gt, p_prompt[i].reshape(bp * tp, -1), post_w, final)
        gla_p.append(sg.astype(state_gla.dtype))
        ret_p.append(sr.astype(state_ret.dtype))

        h1, qa, ka, va, la, qr, kr, vr, ra, gr, ga, gt = _pre_call(hs, cos_s, sin_s, pre_w)
        o, sg, sr = _sample_scan_call(qa, ka, la, va, qr, kr, vr,
                                      state_gla[i].astype(F32), state_ret[i].astype(F32), ts)
        hs = _post_call(o, h1, ra, gr, ga, gt, p_sample[i].reshape(bs * ts, -1), post_w, final)
        gla_s.append(sg.astype(state_gla.dtype))
        ret_s.append(sr.astype(state_ret.dtype))

    return (hp.reshape(bp, tp, d), hs.reshape(bs, ts, d),
            jnp.stack(gla_p), jnp.stack(ret_p), jnp.stack(gla_s), jnp.stack(ret_s))
```

```python
import functools
import math

import numpy as np

import jax
import jax.numpy as jnp
from jax import lax
from jax.experimental import pallas as pl
from jax.experimental.pallas import tpu as pltpu

F32 = jnp.float32
BF16 = jnp.bfloat16

N_HEADS = 4
DK = 64
DV = 128
QK = N_HEADS * DK
VW = N_HEADS * DV
N_MAIN = 4 * QK + 4 * VW
LOW_RANK = 16
GATE_TAU = 16.0
ROPE_BASE = 10000.0
PAST_LEN = 16384
REF_CHUNK = 64
EPS = 1e-6
RET_LOG_DECAY = tuple(math.log1p(-(2.0 ** (-5.0 - h))) for h in range(N_HEADS))

PRE_TILE = 256
POST_TILE = 512
SCAN_CHUNK = 128
SAMPLE_SEQS = 8
BF16_ROWS = 16
MXU_TILE = 256
VMEM_LIMIT = 56 * 1024 * 1024
GLA_SAFE_LOG_DECAY = -60.0

NT = (((1,), (1,)), ((), ()))
TN = (((0,), (0,)), ((), ()))


def _dot(a, b):
    return jnp.dot(a, b, preferred_element_type=F32)


def _dot_nt(a, b):
    return lax.dot_general(a, b, NT, preferred_element_type=F32)


def _dot_tn(a, b):
    return lax.dot_general(a, b, TN, preferred_element_type=F32)


def _rms(x, w):
    return x * lax.rsqrt(jnp.mean(x * x, axis=-1, keepdims=True) + EPS) * w


def _sigmoid(x):
    return jax.nn.sigmoid(x)


def _ffn(u, wi_ref, wo_ref):
    d_ff = wo_ref.shape[0]
    cut = -(-d_ff // (2 * MXU_TILE)) * MXU_TILE
    out = None
    for lo, hi in ((0, cut), (cut, d_ff)):
        a = _dot(u, wi_ref[:, lo:hi])
        b = _dot(u, wi_ref[:, d_ff + lo:d_ff + hi])
        g = (a * _sigmoid(a) * b).astype(BF16)
        part = _dot(g, wo_ref[lo:hi, :])
        out = part if out is None else out + part
    return out


def _split3(x):
    hi = x.astype(BF16)
    r1 = x - hi.astype(F32)
    mid = r1.astype(BF16)
    lo = (r1 - mid.astype(F32)).astype(BF16)
    return hi, mid, lo


def _head_lane_const(values, width, per_head):
    lane = lax.broadcasted_iota(jnp.int32, (1, width), 1) // per_head
    out = jnp.full((1, width), values[-1], F32)
    for h in range(len(values) - 2, -1, -1):
        out = jnp.where(lane == h, values[h], out)
    return out


def _head_segment_sum():
    lane_head = lax.broadcasted_iota(jnp.int32, (QK, VW), 0) // DK
    col_head = lax.broadcasted_iota(jnp.int32, (QK, VW), 1) // DV
    return jnp.where(lane_head == col_head, 1.0, 0.0).astype(BF16)


def _swap_halves(x):
    n = x.shape[-1]
    lane = lax.broadcasted_iota(jnp.int32, x.shape, 1)
    fwd = pltpu.roll(x, n - DK // 2, 1)
    bwd = pltpu.roll(x, DK // 2, 1)
    return jnp.where(lane % DK < DK // 2, fwd, bwd)


def _whole(_):
    return pl.BlockSpec(memory_space=pltpu.VMEM)


def _pre_chain(rows, x_ref, cos_ref, sin_ref, n1_ref, w1i_ref, w1o_ref, nm_ref, win_ref, wup_ref, balpha_ref,
               wgate_ref, h1_ref, ra_ref, gr_ref, ga_ref, gt_ref,
               qa_dst, ka_dst, va_dst, la_dst, qr_dst, kr_dst, vr_dst, decay_sums):
    x = x_ref[rows, :]
    u = _rms(x, n1_ref[...]).astype(BF16)
    yield
    f = _ffn(u, w1i_ref, w1o_ref)
    yield
    h = x + 0.5 * f
    h1_ref[rows, :] = h
    u = _rms(h, nm_ref[...]).astype(BF16)
    yield

    main = _dot(u, win_ref[:, :N_MAIN])

    def proj(lo, hi):
        return main[:, lo:hi]

    o = 0
    qa_dst[rows, :] = proj(o, o + QK) * (DK ** -0.5); o += QK
    ka_dst[rows, :] = proj(o, o + QK); o += QK
    va_dst[rows, :] = proj(o, o + VW).astype(BF16); o += VW
    ra = proj(o, o + VW); o += VW
    ra_ref[rows, :] = (ra * _sigmoid(ra)).astype(BF16)
    cos = cos_ref[rows, :]
    sin = sin_ref[rows, :]
    qr = proj(o, o + QK); o += QK
    qr_dst[rows, :] = qr * cos + _swap_halves(qr) * sin
    kr = proj(o, o + QK); o += QK
    kr_dst[rows, :] = (kr * cos + _swap_halves(kr) * sin) * (DK ** -0.5)
    vr_dst[rows, :] = proj(o, o + VW).astype(BF16); o += VW
    gr = proj(o, o + VW); o += VW
    gr_ref[rows, :] = (gr * _sigmoid(gr)).astype(BF16)
    yield
    a_low = _dot(u, win_ref[:, N_MAIN:N_MAIN + LOW_RANK])
    z = _dot(a_low.astype(BF16), wup_ref[...]) + balpha_ref[...]
    log_sig = jnp.minimum(z, 0.0) - jnp.log1p(jnp.exp(-jnp.abs(z)))
    la = log_sig * (1.0 / GATE_TAU)
    la_dst[rows, :] = la
    for lo in range(0, la.shape[0], SCAN_CHUNK):
        decay_sums.append(jnp.min(jnp.sum(la[lo:lo + SCAN_CHUNK, :], axis=0, keepdims=True)))
    d = ga_ref.shape[1]
    gates = _dot(u, wgate_ref[...])
    ga_ref[rows, :] = _sigmoid(gates[:, :d]).astype(BF16)
    gt_ref[rows, :] = _sigmoid(gates[:, d:]).astype(BF16)
    yield


_PRE_ORDER = (0, 0, 1, 0, 1, 0, 1, 0, 1, 1)


def _pre_kernel(x_ref, cos_ref, sin_ref, n1_ref, w1i_ref, w1o_ref, nm_ref, win_ref, wup_ref, balpha_ref,
                h1_ref, qa_ref, ka_ref, va_ref, la_ref, qr_ref, kr_ref, vr_ref, ra_ref, gr_ref, ga_ref, gt_ref,
                wgate_ref):
    @pl.when(pl.program_id(0) == 0)
    def _():
        wgate_ref[...] = win_ref[:, N_MAIN + LOW_RANK:]

    tile = x_ref.shape[0]
    chains = [_pre_chain(slice(lo, lo + tile // 2), x_ref, cos_ref, sin_ref, n1_ref, w1i_ref, w1o_ref, nm_ref,
                         win_ref, wup_ref, balpha_ref, wgate_ref, h1_ref, ra_ref, gr_ref, ga_ref, gt_ref,
                         qa_ref, ka_ref, va_ref, la_ref, qr_ref, kr_ref, vr_ref, [])
              for lo in (0, tile // 2)]
    for c in _PRE_ORDER:
        next(chains[c])


def _pre_call(x, cos, sin, weights):
    n, d = x.shape
    tm = PRE_TILE
    table_blocks = cos.shape[0] // tm
    row = lambda w: pl.BlockSpec((tm, w), lambda i: (i, 0))
    tab = pl.BlockSpec((tm, QK), lambda i: (i % table_blocks, 0))
    out_widths = [(d, F32), (QK, F32), (QK, F32), (VW, BF16), (QK, F32), (QK, F32), (QK, F32),
                  (VW, BF16), (VW, BF16), (VW, BF16), (d, BF16), (d, BF16)]
    return pl.pallas_call(
        _pre_kernel,
        grid=(n // tm,),
        in_specs=[row(d), tab, tab] + [_whole(w) for w in weights],
        out_specs=[row(w) for w, _ in out_widths],
        out_shape=[jax.ShapeDtypeStruct((n, w), t) for w, t in out_widths],
        scratch_shapes=[pltpu.VMEM((d, 2 * d), BF16)],
        compiler_params=pltpu.CompilerParams(
            dimension_semantics=("arbitrary",), vmem_limit_bytes=VMEM_LIMIT),
        name="pre",
    )(x, cos, sin, *weights)


def _front_kernel(x_ref, cos_ref, sin_ref, n1_ref, w1i_ref, w1o_ref, nm_ref, win_ref, wup_ref, balpha_ref,
                  *rest, n_cast, tiles_per_seq):
    cast_in = rest[:n_cast]
    h1_ref, ra_ref, gr_ref, ga_ref, gt_ref, o_ref, sg_ref, sr_ref = rest[n_cast:n_cast + 8]
    cast_out = rest[n_cast + 8:2 * n_cast + 8]
    wgate_ref = rest[2 * n_cast + 8]
    handoff = rest[2 * n_cast + 9:2 * n_cast + 16]
    stg_ref, str_ref, b_ref, acc_ref, ks_ref, bs_ref, vs_ref, decay_ref = rest[2 * n_cast + 16:]
    i = pl.program_id(0)
    slot = i % 2
    qa_w, ka_w, va_w, la_w, qr_w, kr_w, vr_w = (ref.at[slot] for ref in handoff)
    qa_s, ka_s, va_s, la_s, qr_s, kr_s, vr_s = (ref.at[1 - slot] for ref in handoff)
    c_len = SCAN_CHUNK
    tile = x_ref.shape[0]
    n_chunks = tile // c_len
    r = lax.broadcasted_iota(jnp.int32, (c_len, c_len), 0)
    c = lax.broadcasted_iota(jnp.int32, (c_len, c_len), 1)
    causal = c <= r

    def chunk_cumsum(la):
        tril = jnp.where(causal, 1.0, 0.0).astype(BF16)
        hi, mid, lo = _split3(la)
        return _dot(tril, hi) + _dot(tril, mid) + _dot(tril, lo)

    @pl.when(i == 0)
    def _():
        wgate_ref[...] = win_ref[:, N_MAIN + LOW_RANK:]
        for ref in (*handoff, acc_ref):
            ref[...] = jnp.zeros_like(ref)
        decay_ref[0] = 0.0

    @pl.when((i == 0) | (i % tiles_per_seq == 1))
    def _():
        stg_ref[...] = jnp.zeros_like(stg_ref)
        str_ref[...] = jnp.zeros_like(str_ref)

    @pl.when(decay_ref[0] < GLA_SAFE_LOG_DECAY)
    def _():
        seg = _head_segment_sum()
        in_chunk = lax.broadcasted_iota(jnp.int32, (tile, QK), 0) % c_len
        for n in range(n_chunks):
            rows = slice(n * c_len, (n + 1) * c_len)
            b_ref[rows, :] = chunk_cumsum(la_s[rows, :])
        ks_ref[...] = ka_s[...]
        bs_ref[...] = b_ref[...]
        vs_ref[...] = va_s[...].astype(F32)
        acc_ref[...] = jnp.zeros_like(acc_ref)

        def body(d, carry):
            ok = in_chunk >= d
            k_sh, b_sh, v_sh = ks_ref[...], bs_ref[...], vs_ref[...]
            w = jnp.exp(jnp.where(ok, b_ref[...] - b_sh, 0.0))
            p = jnp.where(ok, qa_s[...] * k_sh * w, 0.0).astype(BF16)
            acc_ref[...] += _dot(p, seg) * v_sh
            ks_ref[...] = pltpu.roll(k_sh, 1, 0)
            bs_ref[...] = pltpu.roll(b_sh, 1, 0)
            vs_ref[...] = pltpu.roll(v_sh, 1, 0)
            return carry

        lax.fori_loop(0, c_len, body, 0)

    for src, dst in zip(cast_in, cast_out):
        dst[...] = src[...].astype(BF16)

    def scan():
        exact = decay_ref[0] < GLA_SAFE_LOG_DECAY
        dist = (r - c).astype(F32)
        lg_lane = _head_lane_const(RET_LOG_DECAY, QK, DK)
        row_qk = lax.broadcasted_iota(jnp.int32, (c_len, QK), 0).astype(F32)
        k_decay = jnp.exp(lg_lane * (c_len - 1.0 - row_qk))
        q_decay = jnp.exp(lg_lane * (row_qk + 1.0))
        chunk_decay = jnp.exp(lg_lane * float(c_len))
        gla, ret = [], []
        for n in range(n_chunks):
            rows = slice(n * c_len, (n + 1) * c_len)
            b = chunk_cumsum(la_s[rows, :])
            b_last = b[c_len - 1:c_len, :]
            k = ka_s[rows, :]
            qt = (qa_s[rows, :] * jnp.exp(b)).astype(BF16)
            kt = (k * jnp.exp(jnp.minimum(-b, -GLA_SAFE_LOG_DECAY))).astype(BF16)
            kd = (k * jnp.exp(b_last - b)).astype(BF16)
            gla.append((qt, kt, kd, jnp.exp(b_last), va_s[rows, :]))
            kf = kr_s[rows, :]
            qf = qr_s[rows, :]
            ret.append((qf.astype(BF16), (qf * q_decay).astype(BF16), kf.astype(BF16),
                        (kf * k_decay).astype(BF16), vr_s[rows, :]))
        yield
        for n in range(n_chunks):
            rows = slice(n * c_len, (n + 1) * c_len)
            qt, kt, kd, e_last, v_all = gla[n]
            st = stg_ref[...]
            st_b = st.astype(BF16)
            stg_ref[...] = st * e_last
            for h in range(N_HEADS):
                ks = slice(h * DK, (h + 1) * DK)
                vs = slice(h * DV, (h + 1) * DV)
                v = v_all[:, vs]
                s = jnp.where(causal, _dot_nt(qt[:, ks], kt[:, ks]), 0.0).astype(BF16)
                intra = jnp.where(exact, acc_ref[rows, vs], _dot(s, v))
                o_ref[rows, vs] = intra + _dot_nt(qt[:, ks], st_b[:, ks])
                stg_ref[:, ks] += _dot_tn(v, kd[:, ks])
            yield
        for n in range(n_chunks):
            rows = slice(n * c_len, (n + 1) * c_len)
            q, qd, k, kd, v_all = ret[n]
            st = str_ref[...]
            st_b = st.astype(BF16)
            str_ref[...] = st * chunk_decay
            for h in range(N_HEADS):
                ks = slice(h * DK, (h + 1) * DK)
                vs = slice(h * DV, (h + 1) * DV)
                v = v_all[:, vs]
                lg = RET_LOG_DECAY[h]
                s = (_dot_nt(q[:, ks], k[:, ks]) * jnp.where(causal, jnp.exp(lg * dist), 0.0)).astype(BF16)
                o_ref[rows, VW + h * DV:VW + (h + 1) * DV] = _dot(s, v) + _dot_nt(qd[:, ks], st_b[:, ks])
                str_ref[:, ks] += _dot_tn(v, kd[:, ks])
            yield

    decay_sums = []
    chains = [_pre_chain(slice(lo, lo + tile // FRONT_PRE_CHAINS), x_ref, cos_ref, sin_ref, n1_ref, w1i_ref, w1o_ref,
                         nm_ref, win_ref, wup_ref, balpha_ref, wgate_ref, h1_ref, ra_ref, gr_ref, ga_ref, gt_ref,
                         qa_w, ka_w, va_w, la_w, qr_w, kr_w, vr_w, decay_sums)
              for lo in range(0, tile, tile // FRONT_PRE_CHAINS)]
    chains.append(scan())
    for ch in _FRONT_ORDER:
        next(chains[ch])
    decay_ref[0] = functools.reduce(jnp.minimum, decay_sums)

    @pl.when((i > 0) & (i % tiles_per_seq == 0))
    def _():
        for h in range(N_HEADS):
            ks = slice(h * DK, (h + 1) * DK)
            sg_ref[0, h] = stg_ref[:, ks].T
            sr_ref[0, h] = str_ref[:, ks].T


FRONT_PRE_CHAINS = 1
_FRONT_ORDER = (1, 0, 0, 1, 0, 1, 0, 1, 0, 1)


def _cast_blocks(rows, steps):
    return max(n for n in range(1, steps + 1) if rows % n == 0 and (rows // n) % BF16_ROWS == 0)


def _front_call(x, cos, sin, weights, to_cast, batch, seq):
    n, d = x.shape
    tm = PRE_TILE
    assert tm == 2 * SCAN_CHUNK and seq % tm == 0
    n_tiles = n // tm
    tiles_per_seq = seq // tm
    table_blocks = cos.shape[0] // tm
    cur = lambda i: jnp.minimum(i, n_tiles - 1)
    prev = lambda i: jnp.maximum(i - 1, 0)
    row = lambda w: pl.BlockSpec((tm, w), lambda i: (cur(i), 0))
    tab = pl.BlockSpec((tm, QK), lambda i: (cur(i) % table_blocks, 0))
    state = pl.BlockSpec((1, N_HEADS, DK, DV), lambda i: (prev(i) // tiles_per_seq, 0, 0, 0))
    state_shape = jax.ShapeDtypeStruct((batch, N_HEADS, DK, DV), F32)
    out_widths = [(d, F32), (VW, BF16), (VW, BF16), (d, BF16), (d, BF16)]

    def slab(w):
        blocks = _cast_blocks(w.shape[0], n_tiles)
        return pl.BlockSpec((w.shape[0] // blocks, w.shape[1]), lambda i: (jnp.minimum(i, blocks - 1), 0))

    cast_specs = [slab(w) for w in to_cast]
    vm = lambda rows, w, t: pltpu.VMEM((rows, w), t)
    two = lambda w, t: pltpu.VMEM((2, tm, w), t)
    outs = pl.pallas_call(
        functools.partial(_front_kernel, n_cast=len(to_cast), tiles_per_seq=tiles_per_seq),
        grid=(n_tiles + 1,),
        in_specs=[row(d), tab, tab] + [_whole(w) for w in weights] + cast_specs,
        out_specs=[row(w) for w, _ in out_widths]
        + [pl.BlockSpec((tm, 2 * VW), lambda i: (prev(i), 0)), state, state] + cast_specs,
        out_shape=[jax.ShapeDtypeStruct((n, w), t) for w, t in out_widths]
        + [jax.ShapeDtypeStruct((n, 2 * VW), F32), state_shape, state_shape]
        + [jax.ShapeDtypeStruct(w.shape, BF16) for w in to_cast],
        scratch_shapes=[vm(d, 2 * d, BF16),
                        two(QK, F32), two(QK, F32), two(VW, BF16), two(QK, F32),
                        two(QK, F32), two(QK, F32), two(VW, BF16),
                        vm(DV, QK, F32), vm(DV, QK, F32),
                        vm(tm, QK, F32), vm(tm, VW, F32), vm(tm, QK, F32), vm(tm, QK, F32), vm(tm, VW, F32),
                        pltpu.SMEM((1,), F32)],
        compiler_params=pltpu.CompilerParams(
            dimension_semantics=("arbitrary",), vmem_limit_bytes=VMEM_LIMIT),
        name="front",
    )(x, cos, sin, *weights, *to_cast)
    return outs[:8], outs[8:]


def _sample_scan_kernel(qa_ref, ka_ref, la_ref, va_ref, qr_ref, kr_ref, vr_ref, sg0_ref, sr0_ref,
                        o_ref, sg_ref, sr_ref, *, steps):
    rows = qa_ref.shape[0]
    n_seq = rows // steps
    pair = 8 // steps
    t_qk = lax.broadcasted_iota(jnp.int32, (rows, QK), 0) % steps
    t_v = lax.broadcasted_iota(jnp.int32, (rows, VW), 0) % steps
    seg = _head_segment_sum()

    def down(x, d, t):
        return x if d == 0 else jnp.where(t >= d, pltpu.roll(x, d, 0), 0.0)

    g = la_ref[...]
    b = g
    for d in range(1, steps):
        b = b + down(g, d, t_qk)
    last = jnp.where(t_qk == steps - 1, b, 0.0)
    b_last = last
    for d in range(1, steps):
        b_last = b_last + pltpu.roll(last, rows - d, 0)

    qa = qa_ref[...]
    ka = ka_ref[...]
    va = va_ref[...].astype(F32)
    qr = qr_ref[...]
    kr = kr_ref[...]
    vr = vr_ref[...].astype(F32)
    lg_lane = _head_lane_const(RET_LOG_DECAY, QK, DK)
    lg_v = _head_lane_const(RET_LOG_DECAY, VW, DV)

    oa = jnp.zeros((rows, VW), F32)
    orr = jnp.zeros((rows, VW), F32)
    for d in range(steps):
        ok = t_qk >= d
        w = jnp.where(ok, jnp.exp(jnp.where(ok, b - down(b, d, t_qk), 0.0)), 0.0)
        pa = (qa * down(ka, d, t_qk) * w).astype(BF16)
        oa = oa + _dot(pa, seg) * down(va, d, t_v)
        pr = (qr * down(kr, d, t_qk) * jnp.where(ok, jnp.exp(lg_lane * float(d)), 0.0)).astype(BF16)
        orr = orr + _dot(pr, seg) * down(vr, d, t_v)

    qta = (qa * jnp.exp(b)).astype(BF16)
    kda = (ka * jnp.exp(b_last - b)).astype(BF16)
    e_hi, e_mid, e_lo = _split3(jnp.exp(b_last))
    zero = jnp.zeros_like(e_hi)
    e_parts = jnp.where(t_qk == 0, e_hi, jnp.where(t_qk == 1, e_mid, jnp.where(t_qk == 2, e_lo, zero)))
    qtr = qr.astype(BF16)
    q_scale = jnp.exp(lg_v * (t_v.astype(F32) + 1.0))
    kdr = (kr * jnp.exp(lg_lane * (steps - 1.0 - t_qk.astype(F32)))).astype(BF16)
    ones = jnp.ones((8, DV), BF16)
    slab_row = lax.broadcasted_iota(jnp.int32, (8, 1), 0) // steps
    va_b = va_ref[...]
    vr_b = vr_ref[...]

    for p in range(n_seq // pair):
        slab = slice(8 * p, 8 * p + 8)
        for h in range(N_HEADS):
            ks = slice(h * DK, (h + 1) * DK)
            vs = slice(h * DV, (h + 1) * DV)
            inter_a = jnp.zeros((8, DV), F32)
            inter_r = jnp.zeros((8, DV), F32)
            for j in range(pair):
                s_idx = p * pair + j
                mine = slab_row == j
                s0a = sg0_ref[s_idx, h]
                s0r = sr0_ref[s_idx, h]
                inter_a = jnp.where(mine, _dot(qta[slab, ks], s0a.astype(BF16)), inter_a)
                inter_r = jnp.where(mine, _dot(qtr[slab, ks], s0r.astype(BF16)), inter_r)
                e_col = _dot_tn(jnp.where(mine, e_parts[slab, ks], 0.0).astype(BF16), ones)
                kd = jnp.where(mine, kda[slab, ks], 0.0).astype(BF16)
                sg_ref[s_idx, h] = e_col * s0a + _dot_tn(kd, va_b[slab, vs])
                kd = jnp.where(mine, kdr[slab, ks], 0.0).astype(BF16)
                sr_ref[s_idx, h] = math.exp(RET_LOG_DECAY[h] * steps) * s0r + _dot_tn(kd, vr_b[slab, vs])
            o_ref[slab, vs] = oa[slab, vs] + inter_a
            o_ref[slab, VW + h * DV:VW + (h + 1) * DV] = orr[slab, vs] + q_scale[slab, vs] * inter_r


def _sample_scan_call(qa, ka, la, va, qr, kr, vr, sg0, sr0, steps):
    n_seq = sg0.shape[0]
    rows = SAMPLE_SEQS * steps
    row = lambda w: pl.BlockSpec((rows, w), lambda i: (i, 0))
    state = pl.BlockSpec((SAMPLE_SEQS, N_HEADS, DK, DV), lambda i: (i, 0, 0, 0))
    state_shape = jax.ShapeDtypeStruct(sg0.shape, F32)
    return pl.pallas_call(
        functools.partial(_sample_scan_kernel, steps=steps),
        grid=(n_seq // SAMPLE_SEQS,),
        in_specs=[row(QK), row(QK), row(QK), row(VW), row(QK), row(QK), row(VW), state, state],
        out_specs=[row(2 * VW), state, state],
        out_shape=[jax.ShapeDtypeStruct((n_seq * steps, 2 * VW), F32), state_shape, state_shape],
        compiler_params=pltpu.CompilerParams(
            dimension_semantics=("arbitrary",), vmem_limit_bytes=VMEM_LIMIT),
        name="sample_scan",
    )(qa, ka, la, va, qr, kr, vr, sg0, sr0)


def _post_chain(rows, o_ref, h1_ref, ra_ref, gr_ref, ga_ref, gt_ref, p_ref,
                gn_ref, wout_ref, n2_ref, w2i_ref, w2o_ref, npl_ref, wpg_ref, wpp_ref, nf_ref, y_ref, final):
    o = o_ref[rows, :]
    normed = []
    for h in range(2 * N_HEADS):
        oh = o[:, h * DV:(h + 1) * DV]
        normed.append(oh * lax.rsqrt(jnp.mean(oh * oh, axis=-1, keepdims=True) + EPS))
    on = jnp.concatenate(normed, axis=-1) * gn_ref[...]
    oa = (on[:, :VW] * ra_ref[rows, :].astype(F32)).astype(BF16)
    orr = (on[:, VW:] * gr_ref[rows, :].astype(F32)).astype(BF16)
    yield
    mix = (ga_ref[rows, :].astype(F32) * _dot(oa, wout_ref[:VW, :])
           + gt_ref[rows, :].astype(F32) * _dot(orr, wout_ref[VW:, :]))
    yield
    h = h1_ref[rows, :] + mix
    u = _rms(h, n2_ref[...]).astype(BF16)
    yield
    f = _ffn(u, w2i_ref, w2o_ref)
    yield
    h = h + 0.5 * f
    u = _rms(h, npl_ref[...]).astype(BF16)
    yield
    gate = _dot(u, wpg_ref[...])
    proj = _dot(p_ref[rows, :].astype(BF16), wpp_ref[...])
    yield
    h = h + proj * _sigmoid(gate)
    y_ref[rows, :] = _rms(h, nf_ref[...]) if final else h
    yield


_POST_ORDER = (0, 0, 1, 0, 1, 0, 1, 1, 0, 0, 1, 0, 1, 1)


def _post_kernel(*refs, final):
    tile = refs[0].shape[0]
    chains = [_post_chain(slice(lo, lo + tile // 2), *refs, final) for lo in (0, tile // 2)]
    for c in _POST_ORDER:
        next(chains[c])


def _post_call(o, h1, ra, gr, ga, gt, p, weights, final):
    n, d = h1.shape
    tm = POST_TILE
    row = lambda w: pl.BlockSpec((tm, w), lambda i: (i, 0))
    return pl.pallas_call(
        functools.partial(_post_kernel, final=final),
        grid=(n // tm,),
        in_specs=[row(2 * VW), row(d), row(VW), row(VW), row(d), row(d), row(p.shape[1])]
        + [_whole(w) for w in weights],
        out_specs=row(d),
        out_shape=jax.ShapeDtypeStruct((n, d), F32),
        compiler_params=pltpu.CompilerParams(
            dimension_semantics=("arbitrary",), vmem_limit_bytes=VMEM_LIMIT),
        name="post",
    )(o, h1, ra, gr, ga, gt, p, *weights)


def _rotary_tables(pos, rows):
    half = DK // 2
    freq = ROPE_BASE ** (-np.arange(half, dtype=np.float64) / half)
    ang = np.asarray(pos, np.float64)[:, None] * freq[None, :]
    cos, sin = np.cos(ang), np.sin(ang)
    reps = (rows // len(pos), N_HEADS)
    return (jnp.asarray(np.tile(np.concatenate([cos, cos], axis=-1), reps), F32),
            jnp.asarray(np.tile(np.concatenate([-sin, sin], axis=-1), reps), F32))


def kernel(x_prompt, x_sample, state_gla, state_ret, p_prompt, p_sample, norm_ffn1, w_ffn1_in, w_ffn1_out, norm_mix, w_in, w_alpha_up, b_alpha, gn_gla, gn_ret, w_out, norm_ffn2, w_ffn2_in, w_ffn2_out, norm_ple, w_ple_gate, w_ple_proj, norm_final):
    bp, tp, d = x_prompt.shape
    bs, ts, _ = x_sample.shape
    depth = w_in.shape[0]
    assert w_in.shape[2] == N_MAIN + LOW_RANK + 2 * d
    assert tp % PRE_TILE == 0 and tp % POST_TILE == 0 and tp % REF_CHUNK == 0 and 3 <= ts < REF_CHUNK and 8 % ts == 0
    assert (bs * ts) % POST_TILE == 0 and (bs * ts) % PRE_TILE == 0 and bs % SAMPLE_SEQS == 0

    cos_p, sin_p = _rotary_tables(np.arange(tp), tp)
    cos_s, sin_s = _rotary_tables(PAST_LEN + np.arange(ts), PRE_TILE)

    hp = x_prompt.reshape(bp * tp, d)
    hs = x_sample.reshape(bs * ts, d)
    row = lambda v: v.reshape(1, -1).astype(F32)
    gla_p, ret_p, gla_s, ret_s = [], [], [], []
    for i in range(depth):
        pre_w = (row(norm_ffn1[i]), w_ffn1_in[i].astype(BF16), w_ffn1_out[i].astype(BF16), row(norm_mix[i]),
                 w_in[i].astype(BF16), w_alpha_up[i].astype(BF16), row(b_alpha[i]))
        post_cast = (w_out[i], w_ffn2_in[i], w_ffn2_out[i], w_ple_gate[i], w_ple_proj[i])
        final = i == depth - 1

        (h1, ra, gr, ga, gt, o, sg, sr), post_bf = _front_call(hp, cos_p, sin_p, pre_w, post_cast, bp, tp)
        wout_b, w2i_b, w2o_b, wpg_b, wpp_b = post_bf
        post_w = (row(jnp.concatenate([gn_gla[i], gn_ret[i]])), wout_b, row(norm_ffn2[i]), w2i_b, w2o_b,
                  row(norm_ple[i]), wpg_b, wpp_b, row(norm_final))
        hp = _post_call(o, h1, ra, gr, ga, gt, p_prompt[i].reshape(bp * tp, -1), post_w, final)
        gla_p.append(sg.astype(state_gla.dtype))
        ret_p.append(sr.astype(state_ret.dtype))

        h1, qa, ka, va, la, qr, kr, vr, ra, gr, ga, gt = _pre_call(hs, cos_s, sin_s, pre_w)
        o, sg, sr = _sample_scan_call(qa, ka, la, va, qr, kr, vr,
                                      state_gla[i].astype(F32), state_ret[i].astype(F32), ts)
        hs = _post_call(o, h1, ra, gr, ga, gt, p_sample[i].reshape(bs * ts, -1), post_w, final)
        gla_s.append(sg.astype(state_gla.dtype))
        ret_s.append(sr.astype(state_ret.dtype))

    return (hp.reshape(bp, tp, d), hs.reshape(bs, ts, d),
            jnp.stack(gla_p), jnp.stack(ret_p), jnp.stack(gla_s), jnp.stack(ret_s))
```

```python
import functools
import math

import numpy as np

import jax
import jax.numpy as jnp
from jax import lax
from jax.experimental import pallas as pl
from jax.experimental.pallas import tpu as pltpu

F32 = jnp.float32
BF16 = jnp.bfloat16

N_HEADS = 4
DK = 64
DV = 128
QK = N_HEADS * DK
VW = N_HEADS * DV
N_MAIN = 4 * QK + 4 * VW
LOW_RANK = 16
GATE_TAU = 16.0
ROPE_BASE = 10000.0
PAST_LEN = 16384
REF_CHUNK = 64
EPS = 1e-6
RET_LOG_DECAY = tuple(math.log1p(-(2.0 ** (-5.0 - h))) for h in range(N_HEADS))

PRE_TILE = 256
POST_TILE = 512
SCAN_CHUNK = 128
SAMPLE_SEQS = 16
BF16_ROWS = 16
MXU_TILE = 256
VMEM_LIMIT = 56 * 1024 * 1024
GLA_SAFE_LOG_DECAY = -60.0

NT = (((1,), (1,)), ((), ()))
TN = (((0,), (0,)), ((), ()))


def _dot(a, b):
    return jnp.dot(a, b, preferred_element_type=F32)


def _dot_nt(a, b):
    return lax.dot_general(a, b, NT, preferred_element_type=F32)


def _dot_tn(a, b):
    return lax.dot_general(a, b, TN, preferred_element_type=F32)


def _rms(x, w):
    return x * lax.rsqrt(jnp.mean(x * x, axis=-1, keepdims=True) + EPS) * w


def _sigmoid(x):
    return jax.nn.sigmoid(x)


def _ffn(u, wi_ref, wo_ref):
    d_ff = wo_ref.shape[0]
    cut = -(-d_ff // (2 * MXU_TILE)) * MXU_TILE
    out = None
    for lo, hi in ((0, cut), (cut, d_ff)):
        a = _dot(u, wi_ref[:, lo:hi])
        b = _dot(u, wi_ref[:, d_ff + lo:d_ff + hi])
        g = (a * _sigmoid(a) * b).astype(BF16)
        part = _dot(g, wo_ref[lo:hi, :])
        out = part if out is None else out + part
    return out


def _split3(x):
    hi = x.astype(BF16)
    r1 = x - hi.astype(F32)
    mid = r1.astype(BF16)
    lo = (r1 - mid.astype(F32)).astype(BF16)
    return hi, mid, lo


def _head_lane_const(values, width, per_head):
    lane = lax.broadcasted_iota(jnp.int32, (1, width), 1) // per_head
    out = jnp.full((1, width), values[-1], F32)
    for h in range(len(values) - 2, -1, -1):
        out = jnp.where(lane == h, values[h], out)
    return out


def _head_segment_sum():
    lane_head = lax.broadcasted_iota(jnp.int32, (QK, VW), 0) // DK
    col_head = lax.broadcasted_iota(jnp.int32, (QK, VW), 1) // DV
    return jnp.where(lane_head == col_head, 1.0, 0.0).astype(BF16)


def _swap_halves(x):
    n = x.shape[-1]
    lane = lax.broadcasted_iota(jnp.int32, x.shape, 1)
    fwd = pltpu.roll(x, n - DK // 2, 1)
    bwd = pltpu.roll(x, DK // 2, 1)
    return jnp.where(lane % DK < DK // 2, fwd, bwd)


def _gated_norm(o, w, gate):
    return (o * lax.rsqrt(jnp.mean(o * o, axis=-1, keepdims=True) + EPS) * w * gate.astype(F32)).astype(BF16)


def _whole(_):
    return pl.BlockSpec(memory_space=pltpu.VMEM)


def _pre_chain(rows, x_ref, tab_ref, n1_ref, w1i_ref, w1o_ref, nm_ref, win_ref, wup_ref, balpha_ref,
               wgate_ref, h1_ref, gg_ref,
               qa_dst, ka_dst, va_dst, la_dst, qr_dst, kr_dst, vr_dst, ra_dst, gr_dst, decay_sums):
    x = x_ref[rows, :]
    u = _rms(x, n1_ref[...]).astype(BF16)
    yield
    f = _ffn(u, w1i_ref, w1o_ref)
    yield
    h = x + 0.5 * f
    h1_ref[rows, :] = h
    u = _rms(h, nm_ref[...]).astype(BF16)
    yield

    main = _dot(u, win_ref[:, :N_MAIN])

    def proj(lo, hi):
        return main[:, lo:hi]

    o = 0
    qa_dst[rows, :] = proj(o, o + QK) * (DK ** -0.5); o += QK
    ka_dst[rows, :] = proj(o, o + QK); o += QK
    va_dst[rows, :] = proj(o, o + VW).astype(BF16); o += VW
    ra = proj(o, o + VW); o += VW
    ra_dst[rows, :] = (ra * _sigmoid(ra)).astype(BF16)
    cos = tab_ref[rows, :QK]
    sin = tab_ref[rows, QK:]
    qr = proj(o, o + QK); o += QK
    qr_dst[rows, :] = qr * cos + _swap_halves(qr) * sin
    kr = proj(o, o + QK); o += QK
    kr_dst[rows, :] = (kr * cos + _swap_halves(kr) * sin) * (DK ** -0.5)
    vr_dst[rows, :] = proj(o, o + VW).astype(BF16); o += VW
    gr = proj(o, o + VW); o += VW
    gr_dst[rows, :] = (gr * _sigmoid(gr)).astype(BF16)
    yield
    a_low = _dot(u, win_ref[:, N_MAIN:N_MAIN + LOW_RANK])
    z = _dot(a_low.astype(BF16), wup_ref[...]) + balpha_ref[...]
    log_sig = jnp.minimum(z, 0.0) - jnp.log1p(jnp.exp(-jnp.abs(z)))
    la = log_sig * (1.0 / GATE_TAU)
    la_dst[rows, :] = la
    for lo in range(0, la.shape[0], SCAN_CHUNK):
        decay_sums.append(jnp.min(jnp.sum(la[lo:lo + SCAN_CHUNK, :], axis=0, keepdims=True)))
    gg_ref[rows, :] = _sigmoid(_dot(u, wgate_ref[...])).astype(BF16)
    yield


_PRE_ORDER = (0, 0, 1, 0, 1, 0, 1, 0, 1, 1)


def _pre_kernel(x_ref, tab_ref, n1_ref, w1i_ref, w1o_ref, nm_ref, win_ref, wup_ref, balpha_ref,
                h1_ref, gg_ref, qa_ref, ka_ref, va_ref, la_ref, qr_ref, kr_ref, vr_ref, ra_ref, gr_ref,
                wgate_ref):
    @pl.when(pl.program_id(0) == 0)
    def _():
        wgate_ref[...] = win_ref[:, N_MAIN + LOW_RANK:]

    tile = x_ref.shape[0]
    chains = [_pre_chain(slice(lo, lo + tile // 2), x_ref, tab_ref, n1_ref, w1i_ref, w1o_ref, nm_ref,
                         win_ref, wup_ref, balpha_ref, wgate_ref, h1_ref, gg_ref,
                         qa_ref, ka_ref, va_ref, la_ref, qr_ref, kr_ref, vr_ref, ra_ref, gr_ref, [])
              for lo in (0, tile // 2)]
    for c in _PRE_ORDER:
        next(chains[c])


def _pre_call(x, table, weights):
    n, d = x.shape
    tm = PRE_TILE
    table_blocks = table.shape[0] // tm
    row = lambda w: pl.BlockSpec((tm, w), lambda i: (i, 0))
    tab = pl.BlockSpec((tm, 2 * QK), lambda i: (i % table_blocks, 0))
    out_widths = [(d, F32), (2 * d, BF16), (QK, F32), (QK, F32), (VW, BF16), (QK, F32), (QK, F32), (QK, F32),
                  (VW, BF16), (VW, BF16), (VW, BF16)]
    return pl.pallas_call(
        _pre_kernel,
        grid=(n // tm,),
        in_specs=[row(d), tab] + [_whole(w) for w in weights],
        out_specs=[row(w) for w, _ in out_widths],
        out_shape=[jax.ShapeDtypeStruct((n, w), t) for w, t in out_widths],
        scratch_shapes=[pltpu.VMEM((d, 2 * d), BF16)],
        compiler_params=pltpu.CompilerParams(
            dimension_semantics=("arbitrary",), vmem_limit_bytes=VMEM_LIMIT),
        name="pre",
    )(x, table, *weights)


def _front_kernel(x_ref, tab_ref, n1_ref, w1i_ref, w1o_ref, nm_ref, win_ref, wup_ref, balpha_ref, gn_ref,
                  *rest, n_cast, tiles_per_seq):
    cast_in = rest[:n_cast]
    h1_ref, gg_ref, o_ref, sg_ref, sr_ref = rest[n_cast:n_cast + 5]
    cast_out = rest[n_cast + 5:2 * n_cast + 5]
    wgate_ref = rest[2 * n_cast + 5]
    handoff = rest[2 * n_cast + 6:2 * n_cast + 15]
    stg_ref, str_ref, b_ref, acc_ref, ks_ref, bs_ref, vs_ref, decay_ref = rest[2 * n_cast + 15:]
    i = pl.program_id(0)
    slot = i % 2
    written = [ref.at[slot] for ref in handoff]
    qa_s, ka_s, va_s, la_s, qr_s, kr_s, vr_s, ra_s, gr_s = (ref.at[1 - slot] for ref in handoff)
    c_len = SCAN_CHUNK
    tile = x_ref.shape[0]
    n_chunks = tile // c_len
    r = lax.broadcasted_iota(jnp.int32, (c_len, c_len), 0)
    c = lax.broadcasted_iota(jnp.int32, (c_len, c_len), 1)
    causal = c <= r

    def chunk_cumsum(la):
        tril = jnp.where(causal, 1.0, 0.0).astype(BF16)
        hi, mid, lo = _split3(la)
        return _dot(tril, hi) + _dot(tril, mid) + _dot(tril, lo)

    @pl.when(i == 0)
    def _():
        wgate_ref[...] = win_ref[:, N_MAIN + LOW_RANK:]
        for ref in (*handoff, acc_ref):
            ref[...] = jnp.zeros_like(ref)
        decay_ref[0] = 0.0

    @pl.when((i == 0) | (i % tiles_per_seq == 1))
    def _():
        stg_ref[...] = jnp.zeros_like(stg_ref)
        str_ref[...] = jnp.zeros_like(str_ref)

    @pl.when(decay_ref[0] < GLA_SAFE_LOG_DECAY)
    def _():
        seg = _head_segment_sum()
        in_chunk = lax.broadcasted_iota(jnp.int32, (tile, QK), 0) % c_len
        for n in range(n_chunks):
            rows = slice(n * c_len, (n + 1) * c_len)
            b_ref[rows, :] = chunk_cumsum(la_s[rows, :])
        ks_ref[...] = ka_s[...]
        bs_ref[...] = b_ref[...]
        vs_ref[...] = va_s[...].astype(F32)
        acc_ref[...] = jnp.zeros_like(acc_ref)

        def body(d, carry):
            ok = in_chunk >= d
            k_sh, b_sh, v_sh = ks_ref[...], bs_ref[...], vs_ref[...]
            w = jnp.exp(jnp.where(ok, b_ref[...] - b_sh, 0.0))
            p = jnp.where(ok, qa_s[...] * k_sh * w, 0.0).astype(BF16)
            acc_ref[...] += _dot(p, seg) * v_sh
            ks_ref[...] = pltpu.roll(k_sh, 1, 0)
            bs_ref[...] = pltpu.roll(b_sh, 1, 0)
            vs_ref[...] = pltpu.roll(v_sh, 1, 0)
            return carry

        lax.fori_loop(0, c_len, body, 0)

    for src, dst in zip(cast_in, cast_out):
        dst[...] = src[...].astype(BF16)

    def scan():
        exact = decay_ref[0] < GLA_SAFE_LOG_DECAY
        dist = (r - c).astype(F32)
        lg_lane = _head_lane_const(RET_LOG_DECAY, QK, DK)
        row_qk = lax.broadcasted_iota(jnp.int32, (c_len, QK), 0).astype(F32)
        k_decay = jnp.exp(lg_lane * (c_len - 1.0 - row_qk))
        q_decay = jnp.exp(lg_lane * (row_qk + 1.0))
        chunk_decay = jnp.exp(lg_lane * float(c_len))
        gla, ret = [], []
        for n in range(n_chunks):
            rows = slice(n * c_len, (n + 1) * c_len)
            b = chunk_cumsum(la_s[rows, :])
            b_last = b[c_len - 1:c_len, :]
            k = ka_s[rows, :]
            qt = (qa_s[rows, :] * jnp.exp(b)).astype(BF16)
            kt = (k * jnp.exp(jnp.minimum(-b, -GLA_SAFE_LOG_DECAY))).astype(BF16)
            kd = (k * jnp.exp(b_last - b)).astype(BF16)
            gla.append((qt, kt, kd, jnp.exp(b_last), va_s[rows, :]))
            kf = kr_s[rows, :]
            qf = qr_s[rows, :]
            ret.append((qf.astype(BF16), (qf * q_decay).astype(BF16), kf.astype(BF16),
                        (kf * k_decay).astype(BF16), vr_s[rows, :]))
        yield
        for n in range(n_chunks):
            rows = slice(n * c_len, (n + 1) * c_len)
            qt, kt, kd, e_last, v_all = gla[n]
            st = stg_ref[...]
            st_b = st.astype(BF16)
            stg_ref[...] = st * e_last
            for h in range(N_HEADS):
                ks = slice(h * DK, (h + 1) * DK)
                vs = slice(h * DV, (h + 1) * DV)
                v = v_all[:, vs]
                s = jnp.where(causal, _dot_nt(qt[:, ks], kt[:, ks]), 0.0).astype(BF16)
                intra = jnp.where(exact, acc_ref[rows, vs], _dot(s, v))
                o_ref[rows, vs] = _gated_norm(intra + _dot_nt(qt[:, ks], st_b[:, ks]), gn_ref[:, vs], ra_s[rows, vs])
                stg_ref[:, ks] += _dot_tn(v, kd[:, ks])
            yield
        for n in range(n_chunks):
            rows = slice(n * c_len, (n + 1) * c_len)
            q, qd, k, kd, v_all = ret[n]
            st = str_ref[...]
            st_b = st.astype(BF16)
            str_ref[...] = st * chunk_decay
            for h in range(N_HEADS):
                ks = slice(h * DK, (h + 1) * DK)
                vs = slice(h * DV, (h + 1) * DV)
                v = v_all[:, vs]
                lg = RET_LOG_DECAY[h]
                s = (_dot_nt(q[:, ks], k[:, ks]) * jnp.where(causal, jnp.exp(lg * dist), 0.0)).astype(BF16)
                cols = slice(VW + h * DV, VW + (h + 1) * DV)
                o_ref[rows, cols] = _gated_norm(_dot(s, v) + _dot_nt(qd[:, ks], st_b[:, ks]),
                                                gn_ref[:, cols], gr_s[rows, vs])
                str_ref[:, ks] += _dot_tn(v, kd[:, ks])
            yield

    last = pl.num_programs(0) - 1

    @pl.when(i < last)
    def _():
        decay_sums = []
        chains = [_pre_chain(slice(lo, lo + tile // FRONT_PRE_CHAINS), x_ref, tab_ref, n1_ref, w1i_ref, w1o_ref,
                             nm_ref, win_ref, wup_ref, balpha_ref, wgate_ref, h1_ref, gg_ref, *written, decay_sums)
                  for lo in range(0, tile, tile // FRONT_PRE_CHAINS)]
        chains.append(scan())
        for ch in _FRONT_ORDER:
            next(chains[ch])
        decay_ref[0] = functools.reduce(jnp.minimum, decay_sums)

    @pl.when(i == last)
    def _():
        for _ in scan():
            pass

    @pl.when((i > 0) & (i % tiles_per_seq == 0))
    def _():
        for h in range(N_HEADS):
            ks = slice(h * DK, (h + 1) * DK)
            sg_ref[0, h] = stg_ref[:, ks].T
            sr_ref[0, h] = str_ref[:, ks].T


FRONT_PRE_CHAINS = 1
_FRONT_ORDER = (1, 0, 0, 1, 0, 1, 0, 1, 0, 1)


def _cast_blocks(rows, steps):
    return max(n for n in range(1, steps + 1) if rows % n == 0 and (rows // n) % BF16_ROWS == 0)


def _front_call(x, table, weights, to_cast, batch, seq):
    n, d = x.shape
    tm = PRE_TILE
    assert tm == 2 * SCAN_CHUNK and seq % tm == 0
    n_tiles = n // tm
    tiles_per_seq = seq // tm
    table_blocks = table.shape[0] // tm
    cur = lambda i: jnp.minimum(i, n_tiles - 1)
    prev = lambda i: jnp.maximum(i - 1, 0)
    row = lambda w: pl.BlockSpec((tm, w), lambda i: (cur(i), 0))
    tab = pl.BlockSpec((tm, 2 * QK), lambda i: (cur(i) % table_blocks, 0))
    state = pl.BlockSpec((1, N_HEADS, DK, DV), lambda i: (prev(i) // tiles_per_seq, 0, 0, 0))
    state_shape = jax.ShapeDtypeStruct((batch, N_HEADS, DK, DV), F32)
    out_widths = [(d, F32), (2 * d, BF16)]

    def slab(w):
        blocks = _cast_blocks(w.shape[0], n_tiles)
        return pl.BlockSpec((w.shape[0] // blocks, w.shape[1]), lambda i: (jnp.minimum(i, blocks - 1), 0))

    cast_specs = [slab(w) for w in to_cast]
    vm = lambda rows, w, t: pltpu.VMEM((rows, w), t)
    two = lambda w, t: pltpu.VMEM((2, tm, w), t)
    outs = pl.pallas_call(
        functools.partial(_front_kernel, n_cast=len(to_cast), tiles_per_seq=tiles_per_seq),
        grid=(n_tiles + 1,),
        in_specs=[row(d), tab] + [_whole(w) for w in weights] + cast_specs,
        out_specs=[row(w) for w, _ in out_widths]
        + [pl.BlockSpec((tm, 2 * VW), lambda i: (prev(i), 0)), state, state] + cast_specs,
        out_shape=[jax.ShapeDtypeStruct((n, w), t) for w, t in out_widths]
        + [jax.ShapeDtypeStruct((n, 2 * VW), BF16), state_shape, state_shape]
        + [jax.ShapeDtypeStruct(w.shape, BF16) for w in to_cast],
        scratch_shapes=[vm(d, 2 * d, BF16),
                        two(QK, F32), two(QK, F32), two(VW, BF16), two(QK, F32),
                        two(QK, F32), two(QK, F32), two(VW, BF16), two(VW, BF16), two(VW, BF16),
                        vm(DV, QK, F32), vm(DV, QK, F32),
                        vm(tm, QK, F32), vm(tm, VW, F32), vm(tm, QK, F32), vm(tm, QK, F32), vm(tm, VW, F32),
                        pltpu.SMEM((1,), F32)],
        compiler_params=pltpu.CompilerParams(
            dimension_semantics=("arbitrary",), vmem_limit_bytes=VMEM_LIMIT),
        name="front",
    )(x, table, *weights, *to_cast)
    return outs[:5], outs[5:]


def _sample_scan_kernel(qa_ref, ka_ref, la_ref, va_ref, qr_ref, kr_ref, vr_ref, ra_ref, gr_ref, gn_ref,
                        sg0_ref, sr0_ref, o_ref, sg_ref, sr_ref, raw_ref, *, steps):
    rows = qa_ref.shape[0]
    n_seq = rows // steps
    pair = 8 // steps
    t_qk = lax.broadcasted_iota(jnp.int32, (rows, QK), 0) % steps
    t_v = lax.broadcasted_iota(jnp.int32, (rows, VW), 0) % steps
    seg = _head_segment_sum()

    def down(x, d, t):
        return x if d == 0 else jnp.where(t >= d, pltpu.roll(x, d, 0), 0.0)

    g = la_ref[...]
    b = g
    for d in range(1, steps):
        b = b + down(g, d, t_qk)
    last = jnp.where(t_qk == steps - 1, b, 0.0)
    b_last = last
    for d in range(1, steps):
        b_last = b_last + pltpu.roll(last, rows - d, 0)

    qa = qa_ref[...]
    ka = ka_ref[...]
    va = va_ref[...].astype(F32)
    qr = qr_ref[...]
    kr = kr_ref[...]
    vr = vr_ref[...].astype(F32)
    lg_lane = _head_lane_const(RET_LOG_DECAY, QK, DK)
    lg_v = _head_lane_const(RET_LOG_DECAY, VW, DV)

    oa = jnp.zeros((rows, VW), F32)
    orr = jnp.zeros((rows, VW), F32)
    for d in range(steps):
        ok = t_qk >= d
        w = jnp.where(ok, jnp.exp(jnp.where(ok, b - down(b, d, t_qk), 0.0)), 0.0)
        pa = (qa * down(ka, d, t_qk) * w).astype(BF16)
        oa = oa + _dot(pa, seg) * down(va, d, t_v)
        pr = (qr * down(kr, d, t_qk) * jnp.where(ok, jnp.exp(lg_lane * float(d)), 0.0)).astype(BF16)
        orr = orr + _dot(pr, seg) * down(vr, d, t_v)

    qta = (qa * jnp.exp(b)).astype(BF16)
    kda = (ka * jnp.exp(b_last - b)).astype(BF16)
    e_hi, e_mid, e_lo = _split3(jnp.exp(b_last))
    zero = jnp.zeros_like(e_hi)
    e_parts = jnp.where(t_qk == 0, e_hi, jnp.where(t_qk == 1, e_mid, jnp.where(t_qk == 2, e_lo, zero)))
    qtr = qr.astype(BF16)
    q_scale = jnp.exp(lg_v * (t_v.astype(F32) + 1.0))
    kdr = (kr * jnp.exp(lg_lane * (steps - 1.0 - t_qk.astype(F32)))).astype(BF16)
    ones = jnp.ones((8, DV), BF16)
    slab_row = lax.broadcasted_iota(jnp.int32, (8, 1), 0) // steps
    va_b = va_ref[...]
    vr_b = vr_ref[...]

    for p in range(n_seq // pair):
        slab = slice(8 * p, 8 * p + 8)
        for h in range(N_HEADS):
            ks = slice(h * DK, (h + 1) * DK)
            vs = slice(h * DV, (h + 1) * DV)
            inter_a = jnp.zeros((8, DV), F32)
            inter_r = jnp.zeros((8, DV), F32)
            for j in range(pair):
                s_idx = p * pair + j
                mine = slab_row == j
                s0a = sg0_ref[s_idx, h]
                s0r = sr0_ref[s_idx, h]
                inter_a = jnp.where(mine, _dot(qta[slab, ks], s0a.astype(BF16)), inter_a)
                inter_r = jnp.where(mine, _dot(qtr[slab, ks], s0r.astype(BF16)), inter_r)
                e_col = _dot_tn(jnp.where(mine, e_parts[slab, ks], 0.0).astype(BF16), ones)
                kd = jnp.where(mine, kda[slab, ks], 0.0).astype(BF16)
                sg_ref[s_idx, h] = e_col * s0a + _dot_tn(kd, va_b[slab, vs])
                kd = jnp.where(mine, kdr[slab, ks], 0.0).astype(BF16)
                sr_ref[s_idx, h] = math.exp(RET_LOG_DECAY[h] * steps) * s0r + _dot_tn(kd, vr_b[slab, vs])
            raw_ref[slab, vs] = oa[slab, vs] + inter_a
            raw_ref[slab, VW + h * DV:VW + (h + 1) * DV] = orr[slab, vs] + q_scale[slab, vs] * inter_r

    for h in range(N_HEADS):
        vs = slice(h * DV, (h + 1) * DV)
        cols = slice(VW + h * DV, VW + (h + 1) * DV)
        o_ref[:, vs] = _gated_norm(raw_ref[:, vs], gn_ref[:, vs], ra_ref[:, vs])
        o_ref[:, cols] = _gated_norm(raw_ref[:, cols], gn_ref[:, cols], gr_ref[:, vs])


def _sample_scan_call(qa, ka, la, va, qr, kr, vr, ra, gr, gn, sg0, sr0, steps):
    n_seq = sg0.shape[0]
    rows = SAMPLE_SEQS * steps
    row = lambda w: pl.BlockSpec((rows, w), lambda i: (i, 0))
    state = pl.BlockSpec((SAMPLE_SEQS, N_HEADS, DK, DV), lambda i: (i, 0, 0, 0))
    state_shape = jax.ShapeDtypeStruct(sg0.shape, F32)
    return pl.pallas_call(
        functools.partial(_sample_scan_kernel, steps=steps),
        grid=(n_seq // SAMPLE_SEQS,),
        in_specs=[row(QK), row(QK), row(QK), row(VW), row(QK), row(QK), row(VW), row(VW), row(VW), _whole(gn),
                  state, state],
        out_specs=[row(2 * VW), state, state],
        out_shape=[jax.ShapeDtypeStruct((n_seq * steps, 2 * VW), BF16), state_shape, state_shape],
        scratch_shapes=[pltpu.VMEM((rows, 2 * VW), F32)],
        compiler_params=pltpu.CompilerParams(
            dimension_semantics=("arbitrary",), vmem_limit_bytes=VMEM_LIMIT),
        name="sample_scan",
    )(qa, ka, la, va, qr, kr, vr, ra, gr, gn, sg0, sr0)


def _post_chain(rows, o_ref, h1_ref, gg_ref, p_ref,
                wout_ref, n2_ref, w2i_ref, w2o_ref, npl_ref, wpg_ref, wpp_ref, nf_ref, y_ref, final):
    d = h1_ref.shape[1]
    mix = (gg_ref[rows, :d].astype(F32) * _dot(o_ref[rows, :VW], wout_ref[:VW, :])
           + gg_ref[rows, d:].astype(F32) * _dot(o_ref[rows, VW:], wout_ref[VW:, :]))
    yield
    h = h1_ref[rows, :] + mix
    u = _rms(h, n2_ref[...]).astype(BF16)
    yield
    f = _ffn(u, w2i_ref, w2o_ref)
    yield
    h = h + 0.5 * f
    u = _rms(h, npl_ref[...]).astype(BF16)
    yield
    gate = _dot(u, wpg_ref[...])
    proj = _dot(p_ref[rows, :].astype(BF16), wpp_ref[...])
    yield
    h = h + proj * _sigmoid(gate)
    y_ref[rows, :] = _rms(h, nf_ref[...]) if final else h
    yield


_POST_ORDER = (0, 0, 1, 0, 1, 1, 0, 0, 1, 1, 0, 1)


def _post_kernel(*refs, final):
    tile = refs[0].shape[0]
    chains = [_post_chain(slice(lo, lo + tile // 2), *refs, final) for lo in (0, tile // 2)]
    for c in _POST_ORDER:
        next(chains[c])


def _post_call(o, h1, gg, p, weights, final):
    n, d = h1.shape
    tm = POST_TILE
    row = lambda w: pl.BlockSpec((tm, w), lambda i: (i, 0))
    return pl.pallas_call(
        functools.partial(_post_kernel, final=final),
        grid=(n // tm,),
        in_specs=[row(2 * VW), row(d), row(2 * d), row(p.shape[1])] + [_whole(w) for w in weights],
        out_specs=row(d),
        out_shape=jax.ShapeDtypeStruct((n, d), F32),
        compiler_params=pltpu.CompilerParams(
            dimension_semantics=("arbitrary",), vmem_limit_bytes=VMEM_LIMIT),
        name="post",
    )(o, h1, gg, p, *weights)


def _rotary_table(pos, rows):
    half = DK // 2
    freq = ROPE_BASE ** (-np.arange(half, dtype=np.float64) / half)
    ang = np.asarray(pos, np.float64)[:, None] * freq[None, :]
    cos, sin = np.cos(ang), np.sin(ang)
    reps = (rows // len(pos), N_HEADS)
    return jnp.asarray(np.concatenate([np.tile(np.concatenate([cos, cos], axis=-1), reps),
                                       np.tile(np.concatenate([-sin, sin], axis=-1), reps)], axis=-1), F32)


def kernel(x_prompt, x_sample, state_gla, state_ret, p_prompt, p_sample, norm_ffn1, w_ffn1_in, w_ffn1_out, norm_mix, w_in, w_alpha_up, b_alpha, gn_gla, gn_ret, w_out, norm_ffn2, w_ffn2_in, w_ffn2_out, norm_ple, w_ple_gate, w_ple_proj, norm_final):
    bp, tp, d = x_prompt.shape
    bs, ts, _ = x_sample.shape
    depth = w_in.shape[0]
    assert w_in.shape[2] == N_MAIN + LOW_RANK + 2 * d
    assert tp % PRE_TILE == 0 and tp % POST_TILE == 0 and tp % REF_CHUNK == 0 and 3 <= ts < REF_CHUNK and 8 % ts == 0
    assert (bs * ts) % POST_TILE == 0 and (bs * ts) % PRE_TILE == 0 and bs % SAMPLE_SEQS == 0

    table_p = _rotary_table(np.arange(tp), tp)
    table_s = _rotary_table(PAST_LEN + np.arange(ts), PRE_TILE)

    hp = x_prompt.reshape(bp * tp, d)
    hs = x_sample.reshape(bs * ts, d)
    row = lambda v: v.reshape(1, -1).astype(F32)
    gla_p, ret_p, gla_s, ret_s = [], [], [], []
    for i in range(depth):
        pre_w = (row(norm_ffn1[i]), w_ffn1_in[i].astype(BF16), w_ffn1_out[i].astype(BF16), row(norm_mix[i]),
                 w_in[i].astype(BF16), w_alpha_up[i].astype(BF16), row(b_alpha[i]))
        post_cast = (w_out[i], w_ffn2_in[i], w_ffn2_out[i], w_ple_gate[i], w_ple_proj[i])
        final = i == depth - 1

        gn = row(jnp.concatenate([gn_gla[i], gn_ret[i]]))
        (h1, gg, o, sg, sr), post_bf = _front_call(hp, table_p, (*pre_w, gn), post_cast, bp, tp)
        wout_b, w2i_b, w2o_b, wpg_b, wpp_b = post_bf
        post_w = (wout_b, row(norm_ffn2[i]), w2i_b, w2o_b, row(norm_ple[i]), wpg_b, wpp_b, row(norm_final))
        hp = _post_call(o, h1, gg, p_prompt[i].reshape(bp * tp, -1), post_w, final)
        gla_p.append(sg.astype(state_gla.dtype))
        ret_p.append(sr.astype(state_ret.dtype))

        h1, gg, qa, ka, va, la, qr, kr, vr, ra, gr = _pre_call(hs, table_s, pre_w)
        o, sg, sr = _sample_scan_call(qa, ka, la, va, qr, kr, vr, ra, gr, gn,
                                      state_gla[i].astype(F32), state_ret[i].astype(F32), ts)
        hs = _post_call(o, h1, gg, p_sample[i].reshape(bs * ts, -1), post_w, final)
        gla_s.append(sg.astype(state_gla.dtype))
        ret_s.append(sr.astype(state_ret.dtype))

    return (hp.reshape(bp, tp, d), hs.reshape(bs, ts, d),
            jnp.stack(gla_p), jnp.stack(ret_p), jnp.stack(gla_s), jnp.stack(ret_s))
```

```python
import functools
import math

import numpy as np

import jax
import jax.numpy as jnp
from jax import lax
from jax.experimental import pallas as pl
from jax.experimental.pallas import tpu as pltpu

F32 = jnp.float32
BF16 = jnp.bfloat16

N_HEADS = 4
DK = 64
DV = 128
QK = N_HEADS * DK
VW = N_HEADS * DV
N_MAIN = 4 * QK + 4 * VW
LOW_RANK = 16
GATE_TAU = 16.0
ROPE_BASE = 10000.0
PAST_LEN = 16384
REF_CHUNK = 64
EPS = 1e-6
RET_LOG_DECAY = tuple(math.log1p(-(2.0 ** (-5.0 - h))) for h in range(N_HEADS))

PRE_TILE = 256
POST_TILE = 512
SCAN_CHUNK = 128
SAMPLE_SEQS = 16
BF16_ROWS = 16
MXU_TILE = 256
VMEM_LIMIT = 56 * 1024 * 1024
GLA_SAFE_LOG_DECAY = -60.0

NT = (((1,), (1,)), ((), ()))
TN = (((0,), (0,)), ((), ()))


def _dot(a, b):
    return jnp.dot(a, b, preferred_element_type=F32)


def _dot_nt(a, b):
    return lax.dot_general(a, b, NT, preferred_element_type=F32)


def _dot_tn(a, b):
    return lax.dot_general(a, b, TN, preferred_element_type=F32)


def _rms(x, w):
    return x * lax.rsqrt(jnp.mean(x * x, axis=-1, keepdims=True) + EPS) * w


def _sigmoid(x):
    return jax.nn.sigmoid(x)


def _ffn(u, wi_ref, wo_ref):
    d_ff = wo_ref.shape[0]
    cut = -(-d_ff // (2 * MXU_TILE)) * MXU_TILE
    out = None
    for lo, hi in ((0, cut), (cut, d_ff)):
        a = _dot(u, wi_ref[:, lo:hi])
        b = _dot(u, wi_ref[:, d_ff + lo:d_ff + hi])
        g = (a * _sigmoid(a) * b).astype(BF16)
        part = _dot(g, wo_ref[lo:hi, :])
        out = part if out is None else out + part
    return out


def _split3(x):
    hi = x.astype(BF16)
    r1 = x - hi.astype(F32)
    mid = r1.astype(BF16)
    lo = (r1 - mid.astype(F32)).astype(BF16)
    return hi, mid, lo


def _head_lane_const(values, width, per_head):
    lane = lax.broadcasted_iota(jnp.int32, (1, width), 1) // per_head
    out = jnp.full((1, width), values[-1], F32)
    for h in range(len(values) - 2, -1, -1):
        out = jnp.where(lane == h, values[h], out)
    return out


def _head_segment_sum():
    lane_head = lax.broadcasted_iota(jnp.int32, (QK, VW), 0) // DK
    col_head = lax.broadcasted_iota(jnp.int32, (QK, VW), 1) // DV
    return jnp.where(lane_head == col_head, 1.0, 0.0).astype(BF16)


def _swap_halves(x):
    n = x.shape[-1]
    lane = lax.broadcasted_iota(jnp.int32, x.shape, 1)
    fwd = pltpu.roll(x, n - DK // 2, 1)
    bwd = pltpu.roll(x, DK // 2, 1)
    return jnp.where(lane % DK < DK // 2, fwd, bwd)


def _gated_norm(o, w, gate):
    return (o * lax.rsqrt(jnp.mean(o * o, axis=-1, keepdims=True) + EPS) * w * gate.astype(F32)).astype(BF16)


def _whole(_):
    return pl.BlockSpec(memory_space=pltpu.VMEM)


def _pre_chain(rows, x_ref, tab_ref, n1_ref, w1i_ref, w1o_ref, nm_ref, win_ref, wup_ref, balpha_ref,
               wgate_ref, h1_ref, gg_ref,
               qa_dst, ka_dst, va_dst, la_dst, qr_dst, kr_dst, vr_dst, ra_dst, gr_dst, decay_sums):
    x = x_ref[rows, :]
    u = _rms(x, n1_ref[...]).astype(BF16)
    yield
    f = _ffn(u, w1i_ref, w1o_ref)
    yield
    h = x + 0.5 * f
    h1_ref[rows, :] = h
    u = _rms(h, nm_ref[...]).astype(BF16)
    yield

    main = _dot(u, win_ref[:, :N_MAIN])

    def proj(lo, hi):
        return main[:, lo:hi]

    o = 0
    qa_dst[rows, :] = proj(o, o + QK) * (DK ** -0.5); o += QK
    ka_dst[rows, :] = proj(o, o + QK); o += QK
    va_dst[rows, :] = proj(o, o + VW).astype(BF16); o += VW
    ra = proj(o, o + VW); o += VW
    ra_dst[rows, :] = (ra * _sigmoid(ra)).astype(BF16)
    cos = tab_ref[rows, :QK]
    sin = tab_ref[rows, QK:]
    qr = proj(o, o + QK); o += QK
    qr_dst[rows, :] = qr * cos + _swap_halves(qr) * sin
    kr = proj(o, o + QK); o += QK
    kr_dst[rows, :] = (kr * cos + _swap_halves(kr) * sin) * (DK ** -0.5)
    vr_dst[rows, :] = proj(o, o + VW).astype(BF16); o += VW
    gr = proj(o, o + VW); o += VW
    gr_dst[rows, :] = (gr * _sigmoid(gr)).astype(BF16)
    yield
    a_low = _dot(u, win_ref[:, N_MAIN:N_MAIN + LOW_RANK])
    z = _dot(a_low.astype(BF16), wup_ref[...]) + balpha_ref[...]
    log_sig = jnp.minimum(z, 0.0) - jnp.log1p(jnp.exp(-jnp.abs(z)))
    la = log_sig * (1.0 / GATE_TAU)
    la_dst[rows, :] = la
    for lo in range(0, la.shape[0], SCAN_CHUNK):
        decay_sums.append(jnp.min(jnp.sum(la[lo:lo + SCAN_CHUNK, :], axis=0, keepdims=True)))
    gg_ref[rows, :] = _sigmoid(_dot(u, wgate_ref[...])).astype(BF16)
    yield


_PRE_ORDER = (0, 0, 1, 0, 1, 0, 1, 0, 1, 1)


def _pre_kernel(x_ref, tab_ref, n1_ref, w1i_ref, w1o_ref, nm_ref, win_ref, wup_ref, balpha_ref,
                h1_ref, gg_ref, qa_ref, ka_ref, va_ref, la_ref, qr_ref, kr_ref, vr_ref, ra_ref, gr_ref,
                wgate_ref):
    @pl.when(pl.program_id(0) == 0)
    def _():
        wgate_ref[...] = win_ref[:, N_MAIN + LOW_RANK:]

    tile = x_ref.shape[0]
    chains = [_pre_chain(slice(lo, lo + tile // 2), x_ref, tab_ref, n1_ref, w1i_ref, w1o_ref, nm_ref,
                         win_ref, wup_ref, balpha_ref, wgate_ref, h1_ref, gg_ref,
                         qa_ref, ka_ref, va_ref, la_ref, qr_ref, kr_ref, vr_ref, ra_ref, gr_ref, [])
              for lo in (0, tile // 2)]
    for c in _PRE_ORDER:
        next(chains[c])


def _pre_call(x, table, weights):
    n, d = x.shape
    tm = PRE_TILE
    table_blocks = table.shape[0] // tm
    row = lambda w: pl.BlockSpec((tm, w), lambda i: (i, 0))
    tab = pl.BlockSpec((tm, 2 * QK), lambda i: (i % table_blocks, 0))
    out_widths = [(d, F32), (2 * d, BF16), (QK, F32), (QK, F32), (VW, BF16), (QK, F32), (QK, F32), (QK, F32),
                  (VW, BF16), (VW, BF16), (VW, BF16)]
    return pl.pallas_call(
        _pre_kernel,
        grid=(n // tm,),
        in_specs=[row(d), tab] + [_whole(w) for w in weights],
        out_specs=[row(w) for w, _ in out_widths],
        out_shape=[jax.ShapeDtypeStruct((n, w), t) for w, t in out_widths],
        scratch_shapes=[pltpu.VMEM((d, 2 * d), BF16)],
        compiler_params=pltpu.CompilerParams(
            dimension_semantics=("arbitrary",), vmem_limit_bytes=VMEM_LIMIT),
        name="pre",
    )(x, table, *weights)


def _front_kernel(x_ref, tab_ref, n1_ref, w1i_ref, w1o_ref, nm_ref, win_ref, wup_ref, balpha_ref, gn_ref,
                  *rest, n_cast, tiles_per_seq):
    cast_in = rest[:n_cast]
    h1_ref, gg_ref, o_ref, sg_ref, sr_ref = rest[n_cast:n_cast + 5]
    cast_out = rest[n_cast + 5:2 * n_cast + 5]
    wgate_ref = rest[2 * n_cast + 5]
    handoff = rest[2 * n_cast + 6:2 * n_cast + 15]
    stg_ref, str_ref, b_ref, acc_ref, ks_ref, bs_ref, vs_ref, decay_ref = rest[2 * n_cast + 15:]
    i = pl.program_id(0)
    slot = i % 2
    written = [ref.at[slot] for ref in handoff]
    qa_s, ka_s, va_s, la_s, qr_s, kr_s, vr_s, ra_s, gr_s = (ref.at[1 - slot] for ref in handoff)
    c_len = SCAN_CHUNK
    tile = x_ref.shape[0]
    n_chunks = tile // c_len
    r = lax.broadcasted_iota(jnp.int32, (c_len, c_len), 0)
    c = lax.broadcasted_iota(jnp.int32, (c_len, c_len), 1)
    causal = c <= r

    def chunk_cumsum(la):
        tril = jnp.where(causal, 1.0, 0.0).astype(BF16)
        hi, mid, lo = _split3(la)
        return _dot(tril, hi) + _dot(tril, mid) + _dot(tril, lo)

    @pl.when(i == 0)
    def _():
        wgate_ref[...] = win_ref[:, N_MAIN + LOW_RANK:]
        for ref in (*handoff, acc_ref):
            ref[...] = jnp.zeros_like(ref)
        decay_ref[0] = 0.0

    @pl.when((i == 0) | (i % tiles_per_seq == 1))
    def _():
        stg_ref[...] = jnp.zeros_like(stg_ref)
        str_ref[...] = jnp.zeros_like(str_ref)

    @pl.when(decay_ref[0] < GLA_SAFE_LOG_DECAY)
    def _():
        seg = _head_segment_sum()
        in_chunk = lax.broadcasted_iota(jnp.int32, (tile, QK), 0) % c_len
        for n in range(n_chunks):
            rows = slice(n * c_len, (n + 1) * c_len)
            b_ref[rows, :] = chunk_cumsum(la_s[rows, :])
        ks_ref[...] = ka_s[...]
        bs_ref[...] = b_ref[...]
        vs_ref[...] = va_s[...].astype(F32)
        acc_ref[...] = jnp.zeros_like(acc_ref)

        def body(d, carry):
            ok = in_chunk >= d
            k_sh, b_sh, v_sh = ks_ref[...], bs_ref[...], vs_ref[...]
            w = jnp.exp(jnp.where(ok, b_ref[...] - b_sh, 0.0))
            p = jnp.where(ok, qa_s[...] * k_sh * w, 0.0).astype(BF16)
            acc_ref[...] += _dot(p, seg) * v_sh
            ks_ref[...] = pltpu.roll(k_sh, 1, 0)
            bs_ref[...] = pltpu.roll(b_sh, 1, 0)
            vs_ref[...] = pltpu.roll(v_sh, 1, 0)
            return carry

        lax.fori_loop(0, c_len, body, 0)

    for src, dst in zip(cast_in, cast_out):
        dst[...] = src[...].astype(BF16)

    def scan():
        exact = decay_ref[0] < GLA_SAFE_LOG_DECAY
        dist = (r - c).astype(F32)
        lg_lane = _head_lane_const(RET_LOG_DECAY, QK, DK)
        row_qk = lax.broadcasted_iota(jnp.int32, (c_len, QK), 0).astype(F32)
        k_decay = jnp.exp(lg_lane * (c_len - 1.0 - row_qk))
        q_decay = jnp.exp(lg_lane * (row_qk + 1.0))
        chunk_decay = jnp.exp(lg_lane * float(c_len))
        gla, ret = [], []
        for n in range(n_chunks):
            rows = slice(n * c_len, (n + 1) * c_len)
            b = chunk_cumsum(la_s[rows, :])
            b_last = b[c_len - 1:c_len, :]
            k = ka_s[rows, :]
            qt = (qa_s[rows, :] * jnp.exp(b)).astype(BF16)
            kt = (k * jnp.exp(jnp.minimum(-b, -GLA_SAFE_LOG_DECAY))).astype(BF16)
            kd = (k * jnp.exp(b_last - b)).astype(BF16)
            gla.append((qt, kt, kd, jnp.exp(b_last), va_s[rows, :]))
            kf = kr_s[rows, :]
            qf = qr_s[rows, :]
            ret.append((qf.astype(BF16), (qf * q_decay).astype(BF16), kf.astype(BF16),
                        (kf * k_decay).astype(BF16), vr_s[rows, :]))
        yield
        for n in range(n_chunks):
            rows = slice(n * c_len, (n + 1) * c_len)
            qt, kt, kd, e_last, v_all = gla[n]
            st = stg_ref[...]
            st_b = st.astype(BF16)
            stg_ref[...] = st * e_last
            for h in range(N_HEADS):
                ks = slice(h * DK, (h + 1) * DK)
                vs = slice(h * DV, (h + 1) * DV)
                v = v_all[:, vs]
                s = jnp.where(causal, _dot_nt(qt[:, ks], kt[:, ks]), 0.0).astype(BF16)
                intra = jnp.where(exact, acc_ref[rows, vs], _dot(s, v))
                o_ref[rows, vs] = _gated_norm(intra + _dot_nt(qt[:, ks], st_b[:, ks]), gn_ref[:, vs], ra_s[rows, vs])
                stg_ref[:, ks] += _dot_tn(v, kd[:, ks])
            yield
        for n in range(n_chunks):
            rows = slice(n * c_len, (n + 1) * c_len)
            q, qd, k, kd, v_all = ret[n]
            st = str_ref[...]
            st_b = st.astype(BF16)
            str_ref[...] = st * chunk_decay
            for h in range(N_HEADS):
                ks = slice(h * DK, (h + 1) * DK)
                vs = slice(h * DV, (h + 1) * DV)
                v = v_all[:, vs]
                lg = RET_LOG_DECAY[h]
                s = (_dot_nt(q[:, ks], k[:, ks]) * jnp.where(causal, jnp.exp(lg * dist), 0.0)).astype(BF16)
                cols = slice(VW + h * DV, VW + (h + 1) * DV)
                o_ref[rows, cols] = _gated_norm(_dot(s, v) + _dot_nt(qd[:, ks], st_b[:, ks]),
                                                gn_ref[:, cols], gr_s[rows, vs])
                str_ref[:, ks] += _dot_tn(v, kd[:, ks])
            yield

    last = pl.num_programs(0) - 1

    @pl.when(i < last)
    def _():
        decay_sums = []
        chains = [_pre_chain(slice(lo, lo + tile // FRONT_PRE_CHAINS), x_ref, tab_ref, n1_ref, w1i_ref, w1o_ref,
                             nm_ref, win_ref, wup_ref, balpha_ref, wgate_ref, h1_ref, gg_ref, *written, decay_sums)
                  for lo in range(0, tile, tile // FRONT_PRE_CHAINS)]
        chains.append(scan())
        for ch in _FRONT_ORDER:
            next(chains[ch])
        decay_ref[0] = functools.reduce(jnp.minimum, decay_sums)

    @pl.when(i == last)
    def _():
        for _ in scan():
            pass

    @pl.when((i > 0) & (i % tiles_per_seq == 0))
    def _():
        for h in range(N_HEADS):
            ks = slice(h * DK, (h + 1) * DK)
            sg_ref[0, h] = stg_ref[:, ks].T
            sr_ref[0, h] = str_ref[:, ks].T


FRONT_PRE_CHAINS = 1
_FRONT_ORDER = (1, 0, 0, 1, 0, 1, 0, 1, 0, 1)


def _cast_blocks(rows, steps):
    return max(n for n in range(1, steps + 1) if rows % n == 0 and (rows // n) % BF16_ROWS == 0)


def _front_call(x, table, weights, to_cast, batch, seq):
    n, d = x.shape
    tm = PRE_TILE
    assert tm == 2 * SCAN_CHUNK and seq % tm == 0
    n_tiles = n // tm
    tiles_per_seq = seq // tm
    table_blocks = table.shape[0] // tm
    cur = lambda i: jnp.minimum(i, n_tiles - 1)
    prev = lambda i: jnp.maximum(i - 1, 0)
    row = lambda w: pl.BlockSpec((tm, w), lambda i: (cur(i), 0))
    tab = pl.BlockSpec((tm, 2 * QK), lambda i: (cur(i) % table_blocks, 0))
    state = pl.BlockSpec((1, N_HEADS, DK, DV), lambda i: (prev(i) // tiles_per_seq, 0, 0, 0))
    state_shape = jax.ShapeDtypeStruct((batch, N_HEADS, DK, DV), F32)
    out_widths = [(d, F32), (2 * d, BF16)]

    def slab(w):
        blocks = _cast_blocks(w.shape[0], n_tiles)
        return pl.BlockSpec((w.shape[0] // blocks, w.shape[1]), lambda i: (jnp.minimum(i, blocks - 1), 0))

    cast_specs = [slab(w) for w in to_cast]
    vm = lambda rows, w, t: pltpu.VMEM((rows, w), t)
    two = lambda w, t: pltpu.VMEM((2, tm, w), t)
    outs = pl.pallas_call(
        functools.partial(_front_kernel, n_cast=len(to_cast), tiles_per_seq=tiles_per_seq),
        grid=(n_tiles + 1,),
        in_specs=[row(d), tab] + [_whole(w) for w in weights] + cast_specs,
        out_specs=[row(w) for w, _ in out_widths]
        + [pl.BlockSpec((tm, 2 * VW), lambda i: (prev(i), 0)), state, state] + cast_specs,
        out_shape=[jax.ShapeDtypeStruct((n, w), t) for w, t in out_widths]
        + [jax.ShapeDtypeStruct((n, 2 * VW), BF16), state_shape, state_shape]
        + [jax.ShapeDtypeStruct(w.shape, BF16) for w in to_cast],
        scratch_shapes=[vm(d, 2 * d, BF16),
                        two(QK, F32), two(QK, F32), two(VW, BF16), two(QK, F32),
                        two(QK, F32), two(QK, F32), two(VW, BF16), two(VW, BF16), two(VW, BF16),
                        vm(DV, QK, F32), vm(DV, QK, F32),
                        vm(tm, QK, F32), vm(tm, VW, F32), vm(tm, QK, F32), vm(tm, QK, F32), vm(tm, VW, F32),
                        pltpu.SMEM((1,), F32)],
        compiler_params=pltpu.CompilerParams(
            dimension_semantics=("arbitrary",), vmem_limit_bytes=VMEM_LIMIT),
        name="front",
    )(x, table, *weights, *to_cast)
    return outs[:5], outs[5:]


def _sample_scan_kernel(qa_ref, ka_ref, la_ref, va_ref, qr_ref, kr_ref, vr_ref, ra_ref, gr_ref, gn_ref,
                        sg0_ref, sr0_ref, o_ref, sg_ref, sr_ref, raw_ref, *, steps):
    rows = qa_ref.shape[0]
    n_seq = rows // steps
    pair = 8 // steps
    t_qk = lax.broadcasted_iota(jnp.int32, (rows, QK), 0) % steps
    t_v = lax.broadcasted_iota(jnp.int32, (rows, VW), 0) % steps
    seg = _head_segment_sum()

    def down(x, d, t):
        return x if d == 0 else jnp.where(t >= d, pltpu.roll(x, d, 0), 0.0)

    g = la_ref[...]
    b = g
    for d in range(1, steps):
        b = b + down(g, d, t_qk)
    last = jnp.where(t_qk == steps - 1, b, 0.0)
    b_last = last
    for d in range(1, steps):
        b_last = b_last + pltpu.roll(last, rows - d, 0)

    qa = qa_ref[...]
    ka = ka_ref[...]
    va = va_ref[...].astype(F32)
    qr = qr_ref[...]
    kr = kr_ref[...]
    vr = vr_ref[...].astype(F32)
    lg_lane = _head_lane_const(RET_LOG_DECAY, QK, DK)
    lg_v = _head_lane_const(RET_LOG_DECAY, VW, DV)

    oa = jnp.zeros((rows, VW), F32)
    orr = jnp.zeros((rows, VW), F32)
    for d in range(steps):
        ok = t_qk >= d
        w = jnp.where(ok, jnp.exp(jnp.where(ok, b - down(b, d, t_qk), 0.0)), 0.0)
        pa = (qa * down(ka, d, t_qk) * w).astype(BF16)
        oa = oa + _dot(pa, seg) * down(va, d, t_v)
        pr = (qr * down(kr, d, t_qk) * jnp.where(ok, jnp.exp(lg_lane * float(d)), 0.0)).astype(BF16)
        orr = orr + _dot(pr, seg) * down(vr, d, t_v)

    qta = (qa * jnp.exp(b)).astype(BF16)
    kda = (ka * jnp.exp(b_last - b)).astype(BF16)
    e_hi, e_mid, e_lo = _split3(jnp.exp(b_last))
    zero = jnp.zeros_like(e_hi)
    e_parts = jnp.where(t_qk == 0, e_hi, jnp.where(t_qk == 1, e_mid, jnp.where(t_qk == 2, e_lo, zero)))
    qtr = qr.astype(BF16)
    q_scale = jnp.exp(lg_v * (t_v.astype(F32) + 1.0))
    kdr = (kr * jnp.exp(lg_lane * (steps - 1.0 - t_qk.astype(F32)))).astype(BF16)
    ones = jnp.ones((8, DV), BF16)
    slab_row = lax.broadcasted_iota(jnp.int32, (8, 1), 0) // steps
    va_b = va_ref[...]
    vr_b = vr_ref[...]

    for p in range(n_seq // pair):
        slab = slice(8 * p, 8 * p + 8)
        for h in range(N_HEADS):
            ks = slice(h * DK, (h + 1) * DK)
            vs = slice(h * DV, (h + 1) * DV)
            inter_a = jnp.zeros((8, DV), F32)
            inter_r = jnp.zeros((8, DV), F32)
            for j in range(pair):
                s_idx = p * pair + j
                mine = slab_row == j
                s0a = sg0_ref[s_idx, h]
                s0r = sr0_ref[s_idx, h]
                inter_a = jnp.where(mine, _dot(qta[slab, ks], s0a.astype(BF16)), inter_a)
                inter_r = jnp.where(mine, _dot(qtr[slab, ks], s0r.astype(BF16)), inter_r)
                e_col = _dot_tn(jnp.where(mine, e_parts[slab, ks], 0.0).astype(BF16), ones)
                kd = jnp.where(mine, kda[slab, ks], 0.0).astype(BF16)
                sg_ref[s_idx, h] = e_col * s0a + _dot_tn(kd, va_b[slab, vs])
                kd = jnp.where(mine, kdr[slab, ks], 0.0).astype(BF16)
                sr_ref[s_idx, h] = math.exp(RET_LOG_DECAY[h] * steps) * s0r + _dot_tn(kd, vr_b[slab, vs])
            raw_ref[slab, vs] = oa[slab, vs] + inter_a
            raw_ref[slab, VW + h * DV:VW + (h + 1) * DV] = orr[slab, vs] + q_scale[slab, vs] * inter_r

    for h in range(N_HEADS):
        vs = slice(h * DV, (h + 1) * DV)
        cols = slice(VW + h * DV, VW + (h + 1) * DV)
        o_ref[:, vs] = _gated_norm(raw_ref[:, vs], gn_ref[:, vs], ra_ref[:, vs])
        o_ref[:, cols] = _gated_norm(raw_ref[:, cols], gn_ref[:, cols], gr_ref[:, vs])


def _sample_scan_call(qa, ka, la, va, qr, kr, vr, ra, gr, gn, sg0, sr0, steps):
    n_seq = sg0.shape[0]
    rows = SAMPLE_SEQS * steps
    row = lambda w: pl.BlockSpec((rows, w), lambda i: (i, 0))
    state = pl.BlockSpec((SAMPLE_SEQS, N_HEADS, DK, DV), lambda i: (i, 0, 0, 0))
    state_shape = jax.ShapeDtypeStruct(sg0.shape, F32)
    return pl.pallas_call(
        functools.partial(_sample_scan_kernel, steps=steps),
        grid=(n_seq // SAMPLE_SEQS,),
        in_specs=[row(QK), row(QK), row(QK), row(VW), row(QK), row(QK), row(VW), row(VW), row(VW), _whole(gn),
                  state, state],
        out_specs=[row(2 * VW), state, state],
        out_shape=[jax.ShapeDtypeStruct((n_seq * steps, 2 * VW), BF16), state_shape, state_shape],
        scratch_shapes=[pltpu.VMEM((rows, 2 * VW), F32)],
        compiler_params=pltpu.CompilerParams(
            dimension_semantics=("arbitrary",), vmem_limit_bytes=VMEM_LIMIT),
        name="sample_scan",
    )(qa, ka, la, va, qr, kr, vr, ra, gr, gn, sg0, sr0)


def _post_chain(rows, o_ref, h1_ref, gg_ref, p_ref,
                wout_ref, n2_ref, w2i_ref, w2o_ref, npl_ref, wpg_ref, wpp_ref, nf_ref, y_ref, final, u_ref, shared):
    d = h1_ref.shape[1]
    mix = (gg_ref[rows, :d].astype(F32) * _dot(o_ref[rows, :VW], wout_ref[:VW, :])
           + gg_ref[rows, d:].astype(F32) * _dot(o_ref[rows, VW:], wout_ref[VW:, :]))
    yield
    h = h1_ref[rows, :] + mix
    u_ref[rows, :] = _rms(h, n2_ref[...]).astype(BF16)
    yield
    h = h + 0.5 * shared["ffn"][rows, :]
    u = _rms(h, npl_ref[...]).astype(BF16)
    yield
    gate = _dot(u, wpg_ref[...])
    proj = _dot(p_ref[rows, :].astype(BF16), wpp_ref[...])
    yield
    h = h + proj * _sigmoid(gate)
    y_ref[rows, :] = _rms(h, nf_ref[...]) if final else h
    yield


_POST_ORDER = (0, 0, 1, 1, 2, 0, 0, 1, 1, 0, 1)


def _post_kernel(*refs, final):
    *io, u_ref = refs
    tile = io[0].shape[0]
    w2i_ref, w2o_ref = io[6], io[7]
    shared = {}
    chains = [_post_chain(slice(lo, lo + tile // 2), *io, final, u_ref, shared) for lo in (0, tile // 2)]
    for c in _POST_ORDER:
        if c == 2:
            shared["ffn"] = _ffn(u_ref[...], w2i_ref, w2o_ref)
        else:
            next(chains[c])


def _post_call(o, h1, gg, p, weights, final):
    n, d = h1.shape
    tm = POST_TILE
    row = lambda w: pl.BlockSpec((tm, w), lambda i: (i, 0))
    return pl.pallas_call(
        functools.partial(_post_kernel, final=final),
        grid=(n // tm,),
        in_specs=[row(2 * VW), row(d), row(2 * d), row(p.shape[1])] + [_whole(w) for w in weights],
        out_specs=row(d),
        out_shape=jax.ShapeDtypeStruct((n, d), F32),
        scratch_shapes=[pltpu.VMEM((tm, d), BF16)],
        compiler_params=pltpu.CompilerParams(
            dimension_semantics=("arbitrary",), vmem_limit_bytes=VMEM_LIMIT),
        name="post",
    )(o, h1, gg, p, *weights)


def _rotary_table(pos, rows):
    half = DK // 2
    freq = ROPE_BASE ** (-np.arange(half, dtype=np.float64) / half)
    ang = np.asarray(pos, np.float64)[:, None] * freq[None, :]
    cos, sin = np.cos(ang), np.sin(ang)
    reps = (rows // len(pos), N_HEADS)
    return jnp.asarray(np.concatenate([np.tile(np.concatenate([cos, cos], axis=-1), reps),
                                       np.tile(np.concatenate([-sin, sin], axis=-1), reps)], axis=-1), F32)


def kernel(x_prompt, x_sample, state_gla, state_ret, p_prompt, p_sample, norm_ffn1, w_ffn1_in, w_ffn1_out, norm_mix, w_in, w_alpha_up, b_alpha, gn_gla, gn_ret, w_out, norm_ffn2, w_ffn2_in, w_ffn2_out, norm_ple, w_ple_gate, w_ple_proj, norm_final):
    bp, tp, d = x_prompt.shape
    bs, ts, _ = x_sample.shape
    depth = w_in.shape[0]
    assert w_in.shape[2] == N_MAIN + LOW_RANK + 2 * d
    assert tp % PRE_TILE == 0 and tp % POST_TILE == 0 and tp % REF_CHUNK == 0 and 3 <= ts < REF_CHUNK and 8 % ts == 0
    assert (bs * ts) % POST_TILE == 0 and (bs * ts) % PRE_TILE == 0 and bs % SAMPLE_SEQS == 0

    table_p = _rotary_table(np.arange(tp), tp)
    table_s = _rotary_table(PAST_LEN + np.arange(ts), PRE_TILE)

    hp = x_prompt.reshape(bp * tp, d)
    hs = x_sample.reshape(bs * ts, d)
    row = lambda v: v.reshape(1, -1).astype(F32)
    gla_p, ret_p, gla_s, ret_s = [], [], [], []
    for i in range(depth):
        pre_w = (row(norm_ffn1[i]), w_ffn1_in[i].astype(BF16), w_ffn1_out[i].astype(BF16), row(norm_mix[i]),
                 w_in[i].astype(BF16), w_alpha_up[i].astype(BF16), row(b_alpha[i]))
        post_cast = (w_out[i], w_ffn2_in[i], w_ffn2_out[i], w_ple_gate[i], w_ple_proj[i])
        final = i == depth - 1

        gn = row(jnp.concatenate([gn_gla[i], gn_ret[i]]))
        (h1, gg, o, sg, sr), post_bf = _front_call(hp, table_p, (*pre_w, gn), post_cast, bp, tp)
        wout_b, w2i_b, w2o_b, wpg_b, wpp_b = post_bf
        post_w = (wout_b, row(norm_ffn2[i]), w2i_b, w2o_b, row(norm_ple[i]), wpg_b, wpp_b, row(norm_final))
        hp = _post_call(o, h1, gg, p_prompt[i].reshape(bp * tp, -1), post_w, final)
        gla_p.append(sg.astype(state_gla.dtype))
        ret_p.append(sr.astype(state_ret.dtype))

        h1, gg, qa, ka, va, la, qr, kr, vr, ra, gr = _pre_call(hs, table_s, pre_w)
        o, sg, sr = _sample_scan_call(qa, ka, la, va, qr, kr, vr, ra, gr, gn,
                                      state_gla[i].astype(F32), state_ret[i].astype(F32), ts)
        hs = _post_call(o, h1, gg, p_sample[i].reshape(bs * ts, -1), post_w, final)
        gla_s.append(sg.astype(state_gla.dtype))
        ret_s.append(sr.astype(state_ret.dtype))

    return (hp.reshape(bp, tp, d), hs.reshape(bs, ts, d),
            jnp.stack(gla_p), jnp.stack(ret_p), jnp.stack(gla_s), jnp.stack(ret_s))
```

```python
import functools
import math

import numpy as np

import jax
import jax.numpy as jnp
from jax import lax
from jax.experimental import pallas as pl
from jax.experimental.pallas import tpu as pltpu

F32 = jnp.float32
BF16 = jnp.bfloat16

N_HEADS = 4
DK = 64
DV = 128
QK = N_HEADS * DK
VW = N_HEADS * DV
N_MAIN = 4 * QK + 4 * VW
LOW_RANK = 16
GATE_TAU = 16.0
ROPE_BASE = 10000.0
PAST_LEN = 16384
REF_CHUNK = 64
EPS = 1e-6
RET_LOG_DECAY = tuple(math.log1p(-(2.0 ** (-5.0 - h))) for h in range(N_HEADS))

PRE_TILE = 256
POST_TILE = 512
SCAN_CHUNK = 128
SAMPLE_SEQS = 16
BF16_ROWS = 16
MXU_TILE = 256
VMEM_LIMIT = 56 * 1024 * 1024
GLA_SAFE_LOG_DECAY = -60.0

NT = (((1,), (1,)), ((), ()))
TN = (((0,), (0,)), ((), ()))


def _dot(a, b):
    return jnp.dot(a, b, preferred_element_type=F32)


def _dot_nt(a, b):
    return lax.dot_general(a, b, NT, preferred_element_type=F32)


def _dot_tn(a, b):
    return lax.dot_general(a, b, TN, preferred_element_type=F32)


def _rms(x, w):
    return x * lax.rsqrt(jnp.mean(x * x, axis=-1, keepdims=True) + EPS) * w


def _sigmoid(x):
    return jax.nn.sigmoid(x)


def _ffn(u, wi_ref, wo_ref):
    d_ff = wo_ref.shape[0]
    cut = -(-d_ff // (2 * MXU_TILE)) * MXU_TILE
    out = None
    for lo, hi in ((0, cut), (cut, d_ff)):
        a = _dot(u, wi_ref[:, lo:hi])
        b = _dot(u, wi_ref[:, d_ff + lo:d_ff + hi])
        g = (a * _sigmoid(a) * b).astype(BF16)
        part = _dot(g, wo_ref[lo:hi, :])
        out = part if out is None else out + part
    return out


def _split3(x):
    hi = x.astype(BF16)
    r1 = x - hi.astype(F32)
    mid = r1.astype(BF16)
    lo = (r1 - mid.astype(F32)).astype(BF16)
    return hi, mid, lo


def _head_lane_const(values, width, per_head):
    lane = lax.broadcasted_iota(jnp.int32, (1, width), 1) // per_head
    out = jnp.full((1, width), values[-1], F32)
    for h in range(len(values) - 2, -1, -1):
        out = jnp.where(lane == h, values[h], out)
    return out


def _head_segment_sum():
    lane_head = lax.broadcasted_iota(jnp.int32, (QK, VW), 0) // DK
    col_head = lax.broadcasted_iota(jnp.int32, (QK, VW), 1) // DV
    return jnp.where(lane_head == col_head, 1.0, 0.0).astype(BF16)


def _swap_halves(x):
    n = x.shape[-1]
    lane = lax.broadcasted_iota(jnp.int32, x.shape, 1)
    fwd = pltpu.roll(x, n - DK // 2, 1)
    bwd = pltpu.roll(x, DK // 2, 1)
    return jnp.where(lane % DK < DK // 2, fwd, bwd)


def _gated_norm(o, w, gate):
    return (o * lax.rsqrt(jnp.mean(o * o, axis=-1, keepdims=True) + EPS) * w * gate.astype(F32)).astype(BF16)


def _whole(_):
    return pl.BlockSpec(memory_space=pltpu.VMEM)


def _pre_chain(rows, x_ref, tab_ref, n1_ref, w1i_ref, w1o_ref, nm_ref, win_ref, wup_ref, balpha_ref,
               wgate_ref, h1_ref, gg_ref,
               qa_dst, ka_dst, va_dst, la_dst, qr_dst, kr_dst, vr_dst, ra_dst, gr_dst, decay_sums):
    x = x_ref[rows, :]
    u = _rms(x, n1_ref[...]).astype(BF16)
    yield
    f = _ffn(u, w1i_ref, w1o_ref)
    yield
    h = x + 0.5 * f
    h1_ref[rows, :] = h
    u = _rms(h, nm_ref[...]).astype(BF16)
    yield

    main = _dot(u, win_ref[:, :N_MAIN])

    def proj(lo, hi):
        return main[:, lo:hi]

    o = 0
    qa_dst[rows, :] = proj(o, o + QK) * (DK ** -0.5); o += QK
    ka_dst[rows, :] = proj(o, o + QK); o += QK
    va_dst[rows, :] = proj(o, o + VW).astype(BF16); o += VW
    ra = proj(o, o + VW); o += VW
    ra_dst[rows, :] = (ra * _sigmoid(ra)).astype(BF16)
    cos = tab_ref[rows, :QK]
    sin = tab_ref[rows, QK:]
    qr = proj(o, o + QK); o += QK
    qr_dst[rows, :] = qr * cos + _swap_halves(qr) * sin
    kr = proj(o, o + QK); o += QK
    kr_dst[rows, :] = (kr * cos + _swap_halves(kr) * sin) * (DK ** -0.5)
    vr_dst[rows, :] = proj(o, o + VW).astype(BF16); o += VW
    gr = proj(o, o + VW); o += VW
    gr_dst[rows, :] = (gr * _sigmoid(gr)).astype(BF16)
    yield
    a_low = _dot(u, win_ref[:, N_MAIN:N_MAIN + LOW_RANK])
    z = _dot(a_low.astype(BF16), wup_ref[...]) + balpha_ref[...]
    log_sig = jnp.minimum(z, 0.0) - jnp.log1p(jnp.exp(-jnp.abs(z)))
    la = log_sig * (1.0 / GATE_TAU)
    la_dst[rows, :] = la
    for lo in range(0, la.shape[0], SCAN_CHUNK):
        decay_sums.append(jnp.min(jnp.sum(la[lo:lo + SCAN_CHUNK, :], axis=0, keepdims=True)))
    gg_ref[rows, :] = _sigmoid(_dot(u, wgate_ref[...])).astype(BF16)
    yield


_PRE_ORDER = (0, 0, 1, 0, 1, 0, 1, 0, 1, 1)


def _pre_kernel(x_ref, tab_ref, n1_ref, w1i_ref, w1o_ref, nm_ref, win_ref, wup_ref, balpha_ref,
                h1_ref, gg_ref, qa_ref, ka_ref, va_ref, la_ref, qr_ref, kr_ref, vr_ref, ra_ref, gr_ref,
                wgate_ref):
    @pl.when(pl.program_id(0) == 0)
    def _():
        wgate_ref[...] = win_ref[:, N_MAIN + LOW_RANK:]

    tile = x_ref.shape[0]
    chains = [_pre_chain(slice(lo, lo + tile // 2), x_ref, tab_ref, n1_ref, w1i_ref, w1o_ref, nm_ref,
                         win_ref, wup_ref, balpha_ref, wgate_ref, h1_ref, gg_ref,
                         qa_ref, ka_ref, va_ref, la_ref, qr_ref, kr_ref, vr_ref, ra_ref, gr_ref, [])
              for lo in (0, tile // 2)]
    for c in _PRE_ORDER:
        next(chains[c])


def _pre_call(x, table, weights):
    n, d = x.shape
    tm = PRE_TILE
    table_blocks = table.shape[0] // tm
    row = lambda w: pl.BlockSpec((tm, w), lambda i: (i, 0))
    tab = pl.BlockSpec((tm, 2 * QK), lambda i: (i % table_blocks, 0))
    out_widths = [(d, F32), (2 * d, BF16), (QK, F32), (QK, F32), (VW, BF16), (QK, F32), (QK, F32), (QK, F32),
                  (VW, BF16), (VW, BF16), (VW, BF16)]
    return pl.pallas_call(
        _pre_kernel,
        grid=(n // tm,),
        in_specs=[row(d), tab] + [_whole(w) for w in weights],
        out_specs=[row(w) for w, _ in out_widths],
        out_shape=[jax.ShapeDtypeStruct((n, w), t) for w, t in out_widths],
        scratch_shapes=[pltpu.VMEM((d, 2 * d), BF16)],
        compiler_params=pltpu.CompilerParams(
            dimension_semantics=("arbitrary",), vmem_limit_bytes=VMEM_LIMIT),
        name="pre",
    )(x, table, *weights)


def _front_kernel(x_ref, tab_ref, n1_ref, w1i_ref, w1o_ref, nm_ref, win_ref, wup_ref, balpha_ref, gn_ref,
                  *rest, n_cast, tiles_per_seq):
    cast_in = rest[:n_cast]
    h1_ref, gg_ref, o_ref, sg_ref, sr_ref = rest[n_cast:n_cast + 5]
    cast_out = rest[n_cast + 5:2 * n_cast + 5]
    wgate_ref = rest[2 * n_cast + 5]
    handoff = rest[2 * n_cast + 6:2 * n_cast + 15]
    stg_ref, str_ref, b_ref, acc_ref, ks_ref, bs_ref, vs_ref, decay_ref = rest[2 * n_cast + 15:]
    i = pl.program_id(0)
    slot = i % 2
    written = [ref.at[slot] for ref in handoff]
    qa_s, ka_s, va_s, la_s, qr_s, kr_s, vr_s, ra_s, gr_s = (ref.at[1 - slot] for ref in handoff)
    c_len = SCAN_CHUNK
    tile = x_ref.shape[0]
    n_chunks = tile // c_len
    r = lax.broadcasted_iota(jnp.int32, (c_len, c_len), 0)
    c = lax.broadcasted_iota(jnp.int32, (c_len, c_len), 1)
    causal = c <= r

    def chunk_cumsum(la):
        tril = jnp.where(causal, 1.0, 0.0).astype(BF16)
        hi, mid, lo = _split3(la)
        return _dot(tril, hi) + _dot(tril, mid) + _dot(tril, lo)

    @pl.when(i == 0)
    def _():
        wgate_ref[...] = win_ref[:, N_MAIN + LOW_RANK:]
        for ref in (*handoff, acc_ref):
            ref[...] = jnp.zeros_like(ref)
        decay_ref[0] = 0.0

    @pl.when((i == 0) | (i % tiles_per_seq == 1))
    def _():
        stg_ref[...] = jnp.zeros_like(stg_ref)
        str_ref[...] = jnp.zeros_like(str_ref)

    @pl.when(decay_ref[0] < GLA_SAFE_LOG_DECAY)
    def _():
        seg = _head_segment_sum()
        in_chunk = lax.broadcasted_iota(jnp.int32, (tile, QK), 0) % c_len
        for n in range(n_chunks):
            rows = slice(n * c_len, (n + 1) * c_len)
            b_ref[rows, :] = chunk_cumsum(la_s[rows, :])
        ks_ref[...] = ka_s[...]
        bs_ref[...] = b_ref[...]
        vs_ref[...] = va_s[...].astype(F32)
        acc_ref[...] = jnp.zeros_like(acc_ref)

        def body(d, carry):
            ok = in_chunk >= d
            k_sh, b_sh, v_sh = ks_ref[...], bs_ref[...], vs_ref[...]
            w = jnp.exp(jnp.where(ok, b_ref[...] - b_sh, 0.0))
            p = jnp.where(ok, qa_s[...] * k_sh * w, 0.0).astype(BF16)
            acc_ref[...] += _dot(p, seg) * v_sh
            ks_ref[...] = pltpu.roll(k_sh, 1, 0)
            bs_ref[...] = pltpu.roll(b_sh, 1, 0)
            vs_ref[...] = pltpu.roll(v_sh, 1, 0)
            return carry

        lax.fori_loop(0, c_len, body, 0)

    for src, dst in zip(cast_in, cast_out):
        dst[...] = src[...].astype(BF16)

    def scan():
        exact = decay_ref[0] < GLA_SAFE_LOG_DECAY
        rr = lax.broadcasted_iota(jnp.int32, (tile, tile), 0)
        cc = lax.broadcasted_iota(jnp.int32, (tile, tile), 1)
        ret_causal = cc <= rr
        dist = (rr - cc).astype(F32)
        lg_lane = _head_lane_const(RET_LOG_DECAY, QK, DK)
        row_qk = lax.broadcasted_iota(jnp.int32, (tile, QK), 0).astype(F32)
        k_decay = jnp.exp(lg_lane * (tile - 1.0 - row_qk))
        q_decay = jnp.exp(lg_lane * (row_qk + 1.0))
        tile_decay = jnp.exp(lg_lane * float(tile))
        gla = []
        for n in range(n_chunks):
            rows = slice(n * c_len, (n + 1) * c_len)
            b = chunk_cumsum(la_s[rows, :])
            b_last = b[c_len - 1:c_len, :]
            k = ka_s[rows, :]
            qt = (qa_s[rows, :] * jnp.exp(b)).astype(BF16)
            kt = (k * jnp.exp(jnp.minimum(-b, -GLA_SAFE_LOG_DECAY))).astype(BF16)
            kd = (k * jnp.exp(b_last - b)).astype(BF16)
            gla.append((qt, kt, kd, jnp.exp(b_last), va_s[rows, :]))
        kf = kr_s[...]
        qf = qr_s[...]
        rq, rqd = qf.astype(BF16), (qf * q_decay).astype(BF16)
        rk, rkd = kf.astype(BF16), (kf * k_decay).astype(BF16)
        rv = vr_s[...]
        yield
        for n in range(n_chunks):
            rows = slice(n * c_len, (n + 1) * c_len)
            qt, kt, kd, e_last, v_all = gla[n]
            st = stg_ref[...]
            st_b = st.astype(BF16)
            stg_ref[...] = st * e_last
            for h in range(N_HEADS):
                ks = slice(h * DK, (h + 1) * DK)
                vs = slice(h * DV, (h + 1) * DV)
                v = v_all[:, vs]
                s = jnp.where(causal, _dot_nt(qt[:, ks], kt[:, ks]), 0.0).astype(BF16)
                intra = jnp.where(exact, acc_ref[rows, vs], _dot(s, v))
                o_ref[rows, vs] = _gated_norm(intra + _dot_nt(qt[:, ks], st_b[:, ks]), gn_ref[:, vs], ra_s[rows, vs])
                stg_ref[:, ks] += _dot_tn(v, kd[:, ks])
            yield
        st = str_ref[...]
        st_b = st.astype(BF16)
        str_ref[...] = st * tile_decay
        for h in range(N_HEADS):
            ks = slice(h * DK, (h + 1) * DK)
            vs = slice(h * DV, (h + 1) * DV)
            v = rv[:, vs]
            lg = RET_LOG_DECAY[h]
            s = (_dot_nt(rq[:, ks], rk[:, ks]) * jnp.where(ret_causal, jnp.exp(lg * dist), 0.0)).astype(BF16)
            cols = slice(VW + h * DV, VW + (h + 1) * DV)
            o_ref[:, cols] = _gated_norm(_dot(s, v) + _dot_nt(rqd[:, ks], st_b[:, ks]),
                                         gn_ref[:, cols], gr_s[:, vs])
            str_ref[:, ks] += _dot_tn(v, rkd[:, ks])
        yield

    last = pl.num_programs(0) - 1

    @pl.when(i < last)
    def _():
        decay_sums = []
        chains = [_pre_chain(slice(lo, lo + tile // FRONT_PRE_CHAINS), x_ref, tab_ref, n1_ref, w1i_ref, w1o_ref,
                             nm_ref, win_ref, wup_ref, balpha_ref, wgate_ref, h1_ref, gg_ref, *written, decay_sums)
                  for lo in range(0, tile, tile // FRONT_PRE_CHAINS)]
        chains.append(scan())
        for ch in _FRONT_ORDER:
            next(chains[ch])
        decay_ref[0] = functools.reduce(jnp.minimum, decay_sums)

    @pl.when(i == last)
    def _():
        for _ in scan():
            pass

    @pl.when((i > 0) & (i % tiles_per_seq == 0))
    def _():
        for h in range(N_HEADS):
            ks = slice(h * DK, (h + 1) * DK)
            sg_ref[0, h] = stg_ref[:, ks].T
            sr_ref[0, h] = str_ref[:, ks].T


FRONT_PRE_CHAINS = 1
_FRONT_ORDER = (1, 0, 0, 1, 0, 1, 0, 1, 0)


def _cast_blocks(rows, steps):
    return max(n for n in range(1, steps + 1) if rows % n == 0 and (rows // n) % BF16_ROWS == 0)


def _front_call(x, table, weights, to_cast, batch, seq):
    n, d = x.shape
    tm = PRE_TILE
    assert tm == 2 * SCAN_CHUNK and seq % tm == 0
    n_tiles = n // tm
    tiles_per_seq = seq // tm
    table_blocks = table.shape[0] // tm
    cur = lambda i: jnp.minimum(i, n_tiles - 1)
    prev = lambda i: jnp.maximum(i - 1, 0)
    row = lambda w: pl.BlockSpec((tm, w), lambda i: (cur(i), 0))
    tab = pl.BlockSpec((tm, 2 * QK), lambda i: (cur(i) % table_blocks, 0))
    state = pl.BlockSpec((1, N_HEADS, DK, DV), lambda i: (prev(i) // tiles_per_seq, 0, 0, 0))
    state_shape = jax.ShapeDtypeStruct((batch, N_HEADS, DK, DV), F32)
    out_widths = [(d, F32), (2 * d, BF16)]

    def slab(w):
        blocks = _cast_blocks(w.shape[0], n_tiles)
        return pl.BlockSpec((w.shape[0] // blocks, w.shape[1]), lambda i: (jnp.minimum(i, blocks - 1), 0))

    cast_specs = [slab(w) for w in to_cast]
    vm = lambda rows, w, t: pltpu.VMEM((rows, w), t)
    two = lambda w, t: pltpu.VMEM((2, tm, w), t)
    outs = pl.pallas_call(
        functools.partial(_front_kernel, n_cast=len(to_cast), tiles_per_seq=tiles_per_seq),
        grid=(n_tiles + 1,),
        in_specs=[row(d), tab] + [_whole(w) for w in weights] + cast_specs,
        out_specs=[row(w) for w, _ in out_widths]
        + [pl.BlockSpec((tm, 2 * VW), lambda i: (prev(i), 0)), state, state] + cast_specs,
        out_shape=[jax.ShapeDtypeStruct((n, w), t) for w, t in out_widths]
        + [jax.ShapeDtypeStruct((n, 2 * VW), BF16), state_shape, state_shape]
        + [jax.ShapeDtypeStruct(w.shape, BF16) for w in to_cast],
        scratch_shapes=[vm(d, 2 * d, BF16),
                        two(QK, F32), two(QK, F32), two(VW, BF16), two(QK, F32),
                        two(QK, F32), two(QK, F32), two(VW, BF16), two(VW, BF16), two(VW, BF16),
                        vm(DV, QK, F32), vm(DV, QK, F32),
                        vm(tm, QK, F32), vm(tm, VW, F32), vm(tm, QK, F32), vm(tm, QK, F32), vm(tm, VW, F32),
                        pltpu.SMEM((1,), F32)],
        compiler_params=pltpu.CompilerParams(
            dimension_semantics=("arbitrary",), vmem_limit_bytes=VMEM_LIMIT),
        name="front",
    )(x, table, *weights, *to_cast)
    return outs[:5], outs[5:]


def _sample_scan_kernel(qa_ref, ka_ref, la_ref, va_ref, qr_ref, kr_ref, vr_ref, ra_ref, gr_ref, gn_ref,
                        sg0_ref, sr0_ref, o_ref, sg_ref, sr_ref, raw_ref, *, steps):
    rows = qa_ref.shape[0]
    n_seq = rows // steps
    pair = 8 // steps
    t_qk = lax.broadcasted_iota(jnp.int32, (rows, QK), 0) % steps
    t_v = lax.broadcasted_iota(jnp.int32, (rows, VW), 0) % steps
    seg = _head_segment_sum()

    def down(x, d, t):
        return x if d == 0 else jnp.where(t >= d, pltpu.roll(x, d, 0), 0.0)

    g = la_ref[...]
    b = g
    for d in range(1, steps):
        b = b + down(g, d, t_qk)
    last = jnp.where(t_qk == steps - 1, b, 0.0)
    b_last = last
    for d in range(1, steps):
        b_last = b_last + pltpu.roll(last, rows - d, 0)

    qa = qa_ref[...]
    ka = ka_ref[...]
    va = va_ref[...].astype(F32)
    qr = qr_ref[...]
    kr = kr_ref[...]
    vr = vr_ref[...].astype(F32)
    lg_lane = _head_lane_const(RET_LOG_DECAY, QK, DK)
    lg_v = _head_lane_const(RET_LOG_DECAY, VW, DV)

    oa = jnp.zeros((rows, VW), F32)
    orr = jnp.zeros((rows, VW), F32)
    for d in range(steps):
        ok = t_qk >= d
        w = jnp.where(ok, jnp.exp(jnp.where(ok, b - down(b, d, t_qk), 0.0)), 0.0)
        pa = (qa * down(ka, d, t_qk) * w).astype(BF16)
        oa = oa + _dot(pa, seg) * down(va, d, t_v)
        pr = (qr * down(kr, d, t_qk) * jnp.where(ok, jnp.exp(lg_lane * float(d)), 0.0)).astype(BF16)
        orr = orr + _dot(pr, seg) * down(vr, d, t_v)

    qta = (qa * jnp.exp(b)).astype(BF16)
    kda = (ka * jnp.exp(b_last - b)).astype(BF16)
    e_hi, e_mid, e_lo = _split3(jnp.exp(b_last))
    zero = jnp.zeros_like(e_hi)
    e_parts = jnp.where(t_qk == 0, e_hi, jnp.where(t_qk == 1, e_mid, jnp.where(t_qk == 2, e_lo, zero)))
    qtr = qr.astype(BF16)
    q_scale = jnp.exp(lg_v * (t_v.astype(F32) + 1.0))
    kdr = (kr * jnp.exp(lg_lane * (steps - 1.0 - t_qk.astype(F32)))).astype(BF16)
    ones = jnp.ones((8, DV), BF16)
    slab_row = lax.broadcasted_iota(jnp.int32, (8, 1), 0) // steps
    va_b = va_ref[...]
    vr_b = vr_ref[...]

    for p in range(n_seq // pair):
        slab = slice(8 * p, 8 * p + 8)
        for h in range(N_HEADS):
            ks = slice(h * DK, (h + 1) * DK)
            vs = slice(h * DV, (h + 1) * DV)
            inter_a = jnp.zeros((8, DV), F32)
            inter_r = jnp.zeros((8, DV), F32)
            for j in range(pair):
                s_idx = p * pair + j
                mine = slab_row == j
                s0a = sg0_ref[s_idx, h]
                s0r = sr0_ref[s_idx, h]
                inter_a = jnp.where(mine, _dot(qta[slab, ks], s0a.astype(BF16)), inter_a)
                inter_r = jnp.where(mine, _dot(qtr[slab, ks], s0r.astype(BF16)), inter_r)
                e_col = _dot_tn(jnp.where(mine, e_parts[slab, ks], 0.0).astype(BF16), ones)
                kd = jnp.where(mine, kda[slab, ks], 0.0).astype(BF16)
                sg_ref[s_idx, h] = e_col * s0a + _dot_tn(kd, va_b[slab, vs])
                kd = jnp.where(mine, kdr[slab, ks], 0.0).astype(BF16)
                sr_ref[s_idx, h] = math.exp(RET_LOG_DECAY[h] * steps) * s0r + _dot_tn(kd, vr_b[slab, vs])
            raw_ref[slab, vs] = oa[slab, vs] + inter_a
            raw_ref[slab, VW + h * DV:VW + (h + 1) * DV] = orr[slab, vs] + q_scale[slab, vs] * inter_r

    for h in range(N_HEADS):
        vs = slice(h * DV, (h + 1) * DV)
        cols = slice(VW + h * DV, VW + (h + 1) * DV)
        o_ref[:, vs] = _gated_norm(raw_ref[:, vs], gn_ref[:, vs], ra_ref[:, vs])
        o_ref[:, cols] = _gated_norm(raw_ref[:, cols], gn_ref[:, cols], gr_ref[:, vs])


def _sample_scan_call(qa, ka, la, va, qr, kr, vr, ra, gr, gn, sg0, sr0, steps):
    n_seq = sg0.shape[0]
    rows = SAMPLE_SEQS * steps
    row = lambda w: pl.BlockSpec((rows, w), lambda i: (i, 0))
    state = pl.BlockSpec((SAMPLE_SEQS, N_HEADS, DK, DV), lambda i: (i, 0, 0, 0))
    state_shape = jax.ShapeDtypeStruct(sg0.shape, F32)
    return pl.pallas_call(
        functools.partial(_sample_scan_kernel, steps=steps),
        grid=(n_seq // SAMPLE_SEQS,),
        in_specs=[row(QK), row(QK), row(QK), row(VW), row(QK), row(QK), row(VW), row(VW), row(VW), _whole(gn),
                  state, state],
        out_specs=[row(2 * VW), state, state],
        out_shape=[jax.ShapeDtypeStruct((n_seq * steps, 2 * VW), BF16), state_shape, state_shape],
        scratch_shapes=[pltpu.VMEM((rows, 2 * VW), F32)],
        compiler_params=pltpu.CompilerParams(
            dimension_semantics=("arbitrary",), vmem_limit_bytes=VMEM_LIMIT),
        name="sample_scan",
    )(qa, ka, la, va, qr, kr, vr, ra, gr, gn, sg0, sr0)


def _post_chain(rows, o_ref, h1_ref, gg_ref, p_ref,
                wout_ref, n2_ref, w2i_ref, w2o_ref, npl_ref, wpg_ref, wpp_ref, nf_ref, y_ref, final):
    d = h1_ref.shape[1]
    mix = (gg_ref[rows, :d].astype(F32) * _dot(o_ref[rows, :VW], wout_ref[:VW, :])
           + gg_ref[rows, d:].astype(F32) * _dot(o_ref[rows, VW:], wout_ref[VW:, :]))
    yield
    h = h1_ref[rows, :] + mix
    u = _rms(h, n2_ref[...]).astype(BF16)
    yield
    f = _ffn(u, w2i_ref, w2o_ref)
    yield
    h = h + 0.5 * f
    u = _rms(h, npl_ref[...]).astype(BF16)
    yield
    gate = _dot(u, wpg_ref[...])
    proj = _dot(p_ref[rows, :].astype(BF16), wpp_ref[...])
    yield
    h = h + proj * _sigmoid(gate)
    y_ref[rows, :] = _rms(h, nf_ref[...]) if final else h
    yield


_POST_ORDER = (0, 0, 1, 0, 1, 1, 0, 0, 1, 1, 0, 1)


def _post_kernel(*refs, final):
    tile = refs[0].shape[0]
    chains = [_post_chain(slice(lo, lo + tile // 2), *refs, final) for lo in (0, tile // 2)]
    for c in _POST_ORDER:
        next(chains[c])


def _post_call(o, h1, gg, p, weights, final):
    n, d = h1.shape
    tm = POST_TILE
    row = lambda w: pl.BlockSpec((tm, w), lambda i: (i, 0))
    return pl.pallas_call(
        functools.partial(_post_kernel, final=final),
        grid=(n // tm,),
        in_specs=[row(2 * VW), row(d), row(2 * d), row(p.shape[1])] + [_whole(w) for w in weights],
        out_specs=row(d),
        out_shape=jax.ShapeDtypeStruct((n, d), F32),
        compiler_params=pltpu.CompilerParams(
            dimension_semantics=("arbitrary",), vmem_limit_bytes=VMEM_LIMIT),
        name="post",
    )(o, h1, gg, p, *weights)


def _rotary_table(pos, rows):
    half = DK // 2
    freq = ROPE_BASE ** (-np.arange(half, dtype=np.float64) / half)
    ang = np.asarray(pos, np.float64)[:, None] * freq[None, :]
    cos, sin = np.cos(ang), np.sin(ang)
    reps = (rows // len(pos), N_HEADS)
    return jnp.asarray(np.concatenate([np.tile(np.concatenate([cos, cos], axis=-1), reps),
                                       np.tile(np.concatenate([-sin, sin], axis=-1), reps)], axis=-1), F32)


def kernel(x_prompt, x_sample, state_gla, state_ret, p_prompt, p_sample, norm_ffn1, w_ffn1_in, w_ffn1_out, norm_mix, w_in, w_alpha_up, b_alpha, gn_gla, gn_ret, w_out, norm_ffn2, w_ffn2_in, w_ffn2_out, norm_ple, w_ple_gate, w_ple_proj, norm_final):
    bp, tp, d = x_prompt.shape
    bs, ts, _ = x_sample.shape
    depth = w_in.shape[0]
    assert w_in.shape[2] == N_MAIN + LOW_RANK + 2 * d
    assert tp % PRE_TILE == 0 and tp % POST_TILE == 0 and tp % REF_CHUNK == 0 and 3 <= ts < REF_CHUNK and 8 % ts == 0
    assert (bs * ts) % POST_TILE == 0 and (bs * ts) % PRE_TILE == 0 and bs % SAMPLE_SEQS == 0

    table_p = _rotary_table(np.arange(tp), tp)
    table_s = _rotary_table(PAST_LEN + np.arange(ts), PRE_TILE)

    hp = x_prompt.reshape(bp * tp, d)
    hs = x_sample.reshape(bs * ts, d)
    row = lambda v: v.reshape(1, -1).astype(F32)
    gla_p, ret_p, gla_s, ret_s = [], [], [], []
    for i in range(depth):
        pre_w = (row(norm_ffn1[i]), w_ffn1_in[i].astype(BF16), w_ffn1_out[i].astype(BF16), row(norm_mix[i]),
                 w_in[i].astype(BF16), w_alpha_up[i].astype(BF16), row(b_alpha[i]))
        post_cast = (w_out[i], w_ffn2_in[i], w_ffn2_out[i], w_ple_gate[i], w_ple_proj[i])
        final = i == depth - 1

        gn = row(jnp.concatenate([gn_gla[i], gn_ret[i]]))
        (h1, gg, o, sg, sr), post_bf = _front_call(hp, table_p, (*pre_w, gn), post_cast, bp, tp)
        wout_b, w2i_b, w2o_b, wpg_b, wpp_b = post_bf
        post_w = (wout_b, row(norm_ffn2[i]), w2i_b, w2o_b, row(norm_ple[i]), wpg_b, wpp_b, row(norm_final))
        hp = _post_call(o, h1, gg, p_prompt[i].reshape(bp * tp, -1), post_w, final)
        gla_p.append(sg.astype(state_gla.dtype))
        ret_p.append(sr.astype(state_ret.dtype))

        h1, gg, qa, ka, va, la, qr, kr, vr, ra, gr = _pre_call(hs, table_s, pre_w)
        o, sg, sr = _sample_scan_call(qa, ka, la, va, qr, kr, vr, ra, gr, gn,
                                      state_gla[i].astype(F32), state_ret[i].astype(F32), ts)
        hs = _post_call(o, h1, gg, p_sample[i].reshape(bs * ts, -1), post_w, final)
        gla_s.append(sg.astype(state_gla.dtype))
        ret_s.append(sr.astype(state_ret.dtype))

    return (hp.reshape(bp, tp, d), hs.reshape(bs, ts, d),
            jnp.stack(gla_p), jnp.stack(ret_p), jnp.stack(gla_s), jnp.stack(ret_s))
```

```python
import functools
import math

import numpy as np

import jax
import jax.numpy as jnp
from jax import lax
from jax.experimental import pallas as pl
from jax.experimental.pallas import tpu as pltpu

F32 = jnp.float32
BF16 = jnp.bfloat16

N_HEADS = 4
DK = 64
DV = 128
QK = N_HEADS * DK
VW = N_HEADS * DV
N_MAIN = 4 * QK + 4 * VW
LOW_RANK = 16
GATE_TAU = 16.0
ROPE_BASE = 10000.0
PAST_LEN = 16384
REF_CHUNK = 64
EPS = 1e-6
RET_LOG_DECAY = tuple(math.log1p(-(2.0 ** (-5.0 - h))) for h in range(N_HEADS))

PRE_TILE = 256
POST_TILE = 512
SCAN_CHUNK = 256
SAMPLE_SEQS = 16
BF16_ROWS = 16
MXU_TILE = 256
VMEM_LIMIT = 56 * 1024 * 1024
GLA_SAFE_LOG_DECAY = -60.0

NT = (((1,), (1,)), ((), ()))
TN = (((0,), (0,)), ((), ()))


def _dot(a, b):
    return jnp.dot(a, b, preferred_element_type=F32)


def _dot_nt(a, b):
    return lax.dot_general(a, b, NT, preferred_element_type=F32)


def _dot_tn(a, b):
    return lax.dot_general(a, b, TN, preferred_element_type=F32)


def _rms(x, w):
    return x * lax.rsqrt(jnp.mean(x * x, axis=-1, keepdims=True) + EPS) * w


def _sigmoid(x):
    return jax.nn.sigmoid(x)


def _ffn(u, wi_ref, wo_ref):
    d_ff = wo_ref.shape[0]
    cut = -(-d_ff // (2 * MXU_TILE)) * MXU_TILE
    out = None
    for lo, hi in ((0, cut), (cut, d_ff)):
        a = _dot(u, wi_ref[:, lo:hi])
        b = _dot(u, wi_ref[:, d_ff + lo:d_ff + hi])
        g = (a * _sigmoid(a) * b).astype(BF16)
        part = _dot(g, wo_ref[lo:hi, :])
        out = part if out is None else out + part
    return out


def _split3(x):
    hi = x.astype(BF16)
    r1 = x - hi.astype(F32)
    mid = r1.astype(BF16)
    lo = (r1 - mid.astype(F32)).astype(BF16)
    return hi, mid, lo


def _head_lane_const(values, width, per_head):
    lane = lax.broadcasted_iota(jnp.int32, (1, width), 1) // per_head
    out = jnp.full((1, width), values[-1], F32)
    for h in range(len(values) - 2, -1, -1):
        out = jnp.where(lane == h, values[h], out)
    return out


def _head_segment_sum():
    lane_head = lax.broadcasted_iota(jnp.int32, (QK, VW), 0) // DK
    col_head = lax.broadcasted_iota(jnp.int32, (QK, VW), 1) // DV
    return jnp.where(lane_head == col_head, 1.0, 0.0).astype(BF16)


def _swap_halves(x):
    n = x.shape[-1]
    lane = lax.broadcasted_iota(jnp.int32, x.shape, 1)
    fwd = pltpu.roll(x, n - DK // 2, 1)
    bwd = pltpu.roll(x, DK // 2, 1)
    return jnp.where(lane % DK < DK // 2, fwd, bwd)


def _gated_norm(o, w, gate):
    return (o * lax.rsqrt(jnp.mean(o * o, axis=-1, keepdims=True) + EPS) * w * gate.astype(F32)).astype(BF16)


def _whole(_):
    return pl.BlockSpec(memory_space=pltpu.VMEM)


def _pre_chain(rows, x_ref, tab_ref, n1_ref, w1i_ref, w1o_ref, nm_ref, win_ref, wup_ref, balpha_ref,
               wgate_ref, h1_ref, gg_ref,
               qa_dst, ka_dst, va_dst, la_dst, qr_dst, kr_dst, vr_dst, ra_dst, gr_dst, decay_sums):
    x = x_ref[rows, :]
    u = _rms(x, n1_ref[...]).astype(BF16)
    yield
    f = _ffn(u, w1i_ref, w1o_ref)
    yield
    h = x + 0.5 * f
    h1_ref[rows, :] = h
    u = _rms(h, nm_ref[...]).astype(BF16)
    yield

    main = _dot(u, win_ref[:, :N_MAIN])

    def proj(lo, hi):
        return main[:, lo:hi]

    o = 0
    qa_dst[rows, :] = proj(o, o + QK) * (DK ** -0.5); o += QK
    ka_dst[rows, :] = proj(o, o + QK); o += QK
    va_dst[rows, :] = proj(o, o + VW).astype(BF16); o += VW
    ra = proj(o, o + VW); o += VW
    ra_dst[rows, :] = (ra * _sigmoid(ra)).astype(BF16)
    cos = tab_ref[rows, :QK]
    sin = tab_ref[rows, QK:]
    qr = proj(o, o + QK); o += QK
    qr_dst[rows, :] = qr * cos + _swap_halves(qr) * sin
    kr = proj(o, o + QK); o += QK
    kr_dst[rows, :] = (kr * cos + _swap_halves(kr) * sin) * (DK ** -0.5)
    vr_dst[rows, :] = proj(o, o + VW).astype(BF16); o += VW
    gr = proj(o, o + VW); o += VW
    gr_dst[rows, :] = (gr * _sigmoid(gr)).astype(BF16)
    yield
    a_low = _dot(u, win_ref[:, N_MAIN:N_MAIN + LOW_RANK])
    z = _dot(a_low.astype(BF16), wup_ref[...]) + balpha_ref[...]
    log_sig = jnp.minimum(z, 0.0) - jnp.log1p(jnp.exp(-jnp.abs(z)))
    la = log_sig * (1.0 / GATE_TAU)
    la_dst[rows, :] = la
    for lo in range(0, la.shape[0], SCAN_CHUNK):
        decay_sums.append(jnp.min(jnp.sum(la[lo:lo + SCAN_CHUNK, :], axis=0, keepdims=True)))
    gg_ref[rows, :] = _sigmoid(_dot(u, wgate_ref[...])).astype(BF16)
    yield


_PRE_ORDER = (0, 0, 1, 0, 1, 0, 1, 0, 1, 1)


def _pre_kernel(x_ref, tab_ref, n1_ref, w1i_ref, w1o_ref, nm_ref, win_ref, wup_ref, balpha_ref,
                h1_ref, gg_ref, qa_ref, ka_ref, va_ref, la_ref, qr_ref, kr_ref, vr_ref, ra_ref, gr_ref,
                wgate_ref):
    @pl.when(pl.program_id(0) == 0)
    def _():
        wgate_ref[...] = win_ref[:, N_MAIN + LOW_RANK:]

    tile = x_ref.shape[0]
    chains = [_pre_chain(slice(lo, lo + tile // 2), x_ref, tab_ref, n1_ref, w1i_ref, w1o_ref, nm_ref,
                         win_ref, wup_ref, balpha_ref, wgate_ref, h1_ref, gg_ref,
                         qa_ref, ka_ref, va_ref, la_ref, qr_ref, kr_ref, vr_ref, ra_ref, gr_ref, [])
              for lo in (0, tile // 2)]
    for c in _PRE_ORDER:
        next(chains[c])


def _pre_call(x, table, weights):
    n, d = x.shape
    tm = PRE_TILE
    table_blocks = table.shape[0] // tm
    row = lambda w: pl.BlockSpec((tm, w), lambda i: (i, 0))
    tab = pl.BlockSpec((tm, 2 * QK), lambda i: (i % table_blocks, 0))
    out_widths = [(d, F32), (2 * d, BF16), (QK, F32), (QK, F32), (VW, BF16), (QK, F32), (QK, F32), (QK, F32),
                  (VW, BF16), (VW, BF16), (VW, BF16)]
    return pl.pallas_call(
        _pre_kernel,
        grid=(n // tm,),
        in_specs=[row(d), tab] + [_whole(w) for w in weights],
        out_specs=[row(w) for w, _ in out_widths],
        out_shape=[jax.ShapeDtypeStruct((n, w), t) for w, t in out_widths],
        scratch_shapes=[pltpu.VMEM((d, 2 * d), BF16)],
        compiler_params=pltpu.CompilerParams(
            dimension_semantics=("arbitrary",), vmem_limit_bytes=VMEM_LIMIT),
        name="pre",
    )(x, table, *weights)


def _front_kernel(x_ref, tab_ref, n1_ref, w1i_ref, w1o_ref, nm_ref, win_ref, wup_ref, balpha_ref, gn_ref,
                  *rest, n_cast, tiles_per_seq):
    cast_in = rest[:n_cast]
    h1_ref, gg_ref, o_ref, sg_ref, sr_ref = rest[n_cast:n_cast + 5]
    cast_out = rest[n_cast + 5:2 * n_cast + 5]
    wgate_ref = rest[2 * n_cast + 5]
    handoff = rest[2 * n_cast + 6:2 * n_cast + 15]
    stg_ref, str_ref, b_ref, acc_ref, ks_ref, bs_ref, vs_ref, decay_ref = rest[2 * n_cast + 15:]
    i = pl.program_id(0)
    slot = i % 2
    written = [ref.at[slot] for ref in handoff]
    qa_s, ka_s, va_s, la_s, qr_s, kr_s, vr_s, ra_s, gr_s = (ref.at[1 - slot] for ref in handoff)
    c_len = SCAN_CHUNK
    tile = x_ref.shape[0]
    n_chunks = tile // c_len
    r = lax.broadcasted_iota(jnp.int32, (c_len, c_len), 0)
    c = lax.broadcasted_iota(jnp.int32, (c_len, c_len), 1)
    causal = c <= r

    def chunk_cumsum(la):
        tril = jnp.where(causal, 1.0, 0.0).astype(BF16)
        hi, mid, lo = _split3(la)
        return _dot(tril, hi) + _dot(tril, mid) + _dot(tril, lo)

    @pl.when(i == 0)
    def _():
        wgate_ref[...] = win_ref[:, N_MAIN + LOW_RANK:]
        for ref in (*handoff, acc_ref):
            ref[...] = jnp.zeros_like(ref)
        decay_ref[0] = 0.0

    @pl.when((i == 0) | (i % tiles_per_seq == 1))
    def _():
        stg_ref[...] = jnp.zeros_like(stg_ref)
        str_ref[...] = jnp.zeros_like(str_ref)

    @pl.when(decay_ref[0] < GLA_SAFE_LOG_DECAY)
    def _():
        seg = _head_segment_sum()
        in_chunk = lax.broadcasted_iota(jnp.int32, (tile, QK), 0) % c_len
        for n in range(n_chunks):
            rows = slice(n * c_len, (n + 1) * c_len)
            b_ref[rows, :] = chunk_cumsum(la_s[rows, :])
        ks_ref[...] = ka_s[...]
        bs_ref[...] = b_ref[...]
        vs_ref[...] = va_s[...].astype(F32)
        acc_ref[...] = jnp.zeros_like(acc_ref)

        def body(d, carry):
            ok = in_chunk >= d
            k_sh, b_sh, v_sh = ks_ref[...], bs_ref[...], vs_ref[...]
            w = jnp.exp(jnp.where(ok, b_ref[...] - b_sh, 0.0))
            p = jnp.where(ok, qa_s[...] * k_sh * w, 0.0).astype(BF16)
            acc_ref[...] += _dot(p, seg) * v_sh
            ks_ref[...] = pltpu.roll(k_sh, 1, 0)
            bs_ref[...] = pltpu.roll(b_sh, 1, 0)
            vs_ref[...] = pltpu.roll(v_sh, 1, 0)
            return carry

        lax.fori_loop(0, c_len, body, 0)

    for src, dst in zip(cast_in, cast_out):
        dst[...] = src[...].astype(BF16)

    def scan():
        exact = decay_ref[0] < GLA_SAFE_LOG_DECAY
        rr = lax.broadcasted_iota(jnp.int32, (tile, tile), 0)
        cc = lax.broadcasted_iota(jnp.int32, (tile, tile), 1)
        ret_causal = cc <= rr
        dist = (rr - cc).astype(F32)
        lg_lane = _head_lane_const(RET_LOG_DECAY, QK, DK)
        row_qk = lax.broadcasted_iota(jnp.int32, (tile, QK), 0).astype(F32)
        k_decay = jnp.exp(lg_lane * (tile - 1.0 - row_qk))
        q_decay = jnp.exp(lg_lane * (row_qk + 1.0))
        tile_decay = jnp.exp(lg_lane * float(tile))
        gla = []
        for n in range(n_chunks):
            rows = slice(n * c_len, (n + 1) * c_len)
            b = chunk_cumsum(la_s[rows, :])
            b_last = b[c_len - 1:c_len, :]
            k = ka_s[rows, :]
            qt = (qa_s[rows, :] * jnp.exp(b)).astype(BF16)
            kt = (k * jnp.exp(jnp.minimum(-b, -GLA_SAFE_LOG_DECAY))).astype(BF16)
            kd = (k * jnp.exp(b_last - b)).astype(BF16)
            gla.append((qt, kt, kd, jnp.exp(b_last), va_s[rows, :]))
        kf = kr_s[...]
        qf = qr_s[...]
        rq, rqd = qf.astype(BF16), (qf * q_decay).astype(BF16)
        rk, rkd = kf.astype(BF16), (kf * k_decay).astype(BF16)
        rv = vr_s[...]
        yield
        for n in range(n_chunks):
            rows = slice(n * c_len, (n + 1) * c_len)
            qt, kt, kd, e_last, v_all = gla[n]
            st = stg_ref[...]
            st_b = st.astype(BF16)
            stg_ref[...] = st * e_last
            for h in range(N_HEADS):
                ks = slice(h * DK, (h + 1) * DK)
                vs = slice(h * DV, (h + 1) * DV)
                v = v_all[:, vs]
                s = jnp.where(causal, _dot_nt(qt[:, ks], kt[:, ks]), 0.0).astype(BF16)
                intra = jnp.where(exact, acc_ref[rows, vs], _dot(s, v))
                o_ref[rows, vs] = _gated_norm(intra + _dot_nt(qt[:, ks], st_b[:, ks]), gn_ref[:, vs], ra_s[rows, vs])
                stg_ref[:, ks] += _dot_tn(v, kd[:, ks])
            yield
        st = str_ref[...]
        st_b = st.astype(BF16)
        str_ref[...] = st * tile_decay
        for h in range(N_HEADS):
            ks = slice(h * DK, (h + 1) * DK)
            vs = slice(h * DV, (h + 1) * DV)
            v = rv[:, vs]
            lg = RET_LOG_DECAY[h]
            s = (_dot_nt(rq[:, ks], rk[:, ks]) * jnp.where(ret_causal, jnp.exp(lg * dist), 0.0)).astype(BF16)
            cols = slice(VW + h * DV, VW + (h + 1) * DV)
            o_ref[:, cols] = _gated_norm(_dot(s, v) + _dot_nt(rqd[:, ks], st_b[:, ks]),
                                         gn_ref[:, cols], gr_s[:, vs])
            str_ref[:, ks] += _dot_tn(v, rkd[:, ks])
        yield

    last = pl.num_programs(0) - 1

    @pl.when(i < last)
    def _():
        decay_sums = []
        chains = [_pre_chain(slice(lo, lo + tile // FRONT_PRE_CHAINS), x_ref, tab_ref, n1_ref, w1i_ref, w1o_ref,
                             nm_ref, win_ref, wup_ref, balpha_ref, wgate_ref, h1_ref, gg_ref, *written, decay_sums)
                  for lo in range(0, tile, tile // FRONT_PRE_CHAINS)]
        chains.append(scan())
        for ch in _FRONT_ORDER:
            next(chains[ch])
        decay_ref[0] = functools.reduce(jnp.minimum, decay_sums)

    @pl.when(i == last)
    def _():
        for _ in scan():
            pass

    @pl.when((i > 0) & (i % tiles_per_seq == 0))
    def _():
        for h in range(N_HEADS):
            ks = slice(h * DK, (h + 1) * DK)
            sg_ref[0, h] = stg_ref[:, ks].T
            sr_ref[0, h] = str_ref[:, ks].T


FRONT_PRE_CHAINS = 1
_FRONT_ORDER = (1, 0, 0, 1, 0, 1, 0, 0)


def _cast_blocks(rows, steps):
    return max(n for n in range(1, steps + 1) if rows % n == 0 and (rows // n) % BF16_ROWS == 0)


def _front_call(x, table, weights, to_cast, batch, seq):
    n, d = x.shape
    tm = PRE_TILE
    assert tm % SCAN_CHUNK == 0 and seq % tm == 0
    n_tiles = n // tm
    tiles_per_seq = seq // tm
    table_blocks = table.shape[0] // tm
    cur = lambda i: jnp.minimum(i, n_tiles - 1)
    prev = lambda i: jnp.maximum(i - 1, 0)
    row = lambda w: pl.BlockSpec((tm, w), lambda i: (cur(i), 0))
    tab = pl.BlockSpec((tm, 2 * QK), lambda i: (cur(i) % table_blocks, 0))
    state = pl.BlockSpec((1, N_HEADS, DK, DV), lambda i: (prev(i) // tiles_per_seq, 0, 0, 0))
    state_shape = jax.ShapeDtypeStruct((batch, N_HEADS, DK, DV), F32)
    out_widths = [(d, F32), (2 * d, BF16)]

    def slab(w):
        blocks = _cast_blocks(w.shape[0], n_tiles)
        return pl.BlockSpec((w.shape[0] // blocks, w.shape[1]), lambda i: (jnp.minimum(i, blocks - 1), 0))

    cast_specs = [slab(w) for w in to_cast]
    vm = lambda rows, w, t: pltpu.VMEM((rows, w), t)
    two = lambda w, t: pltpu.VMEM((2, tm, w), t)
    outs = pl.pallas_call(
        functools.partial(_front_kernel, n_cast=len(to_cast), tiles_per_seq=tiles_per_seq),
        grid=(n_tiles + 1,),
        in_specs=[row(d), tab] + [_whole(w) for w in weights] + cast_specs,
        out_specs=[row(w) for w, _ in out_widths]
        + [pl.BlockSpec((tm, 2 * VW), lambda i: (prev(i), 0)), state, state] + cast_specs,
        out_shape=[jax.ShapeDtypeStruct((n, w), t) for w, t in out_widths]
        + [jax.ShapeDtypeStruct((n, 2 * VW), BF16), state_shape, state_shape]
        + [jax.ShapeDtypeStruct(w.shape, BF16) for w in to_cast],
        scratch_shapes=[vm(d, 2 * d, BF16),
                        two(QK, F32), two(QK, F32), two(VW, BF16), two(QK, F32),
                        two(QK, F32), two(QK, F32), two(VW, BF16), two(VW, BF16), two(VW, BF16),
                        vm(DV, QK, F32), vm(DV, QK, F32),
                        vm(tm, QK, F32), vm(tm, VW, F32), vm(tm, QK, F32), vm(tm, QK, F32), vm(tm, VW, F32),
                        pltpu.SMEM((1,), F32)],
        compiler_params=pltpu.CompilerParams(
            dimension_semantics=("arbitrary",), vmem_limit_bytes=VMEM_LIMIT),
        name="front",
    )(x, table, *weights, *to_cast)
    return outs[:5], outs[5:]


def _sample_scan_kernel(qa_ref, ka_ref, la_ref, va_ref, qr_ref, kr_ref, vr_ref, ra_ref, gr_ref, gn_ref,
                        sg0_ref, sr0_ref, o_ref, sg_ref, sr_ref, raw_ref, *, steps):
    rows = qa_ref.shape[0]
    n_seq = rows // steps
    pair = 8 // steps
    t_qk = lax.broadcasted_iota(jnp.int32, (rows, QK), 0) % steps
    t_v = lax.broadcasted_iota(jnp.int32, (rows, VW), 0) % steps
    seg = _head_segment_sum()

    def down(x, d, t):
        return x if d == 0 else jnp.where(t >= d, pltpu.roll(x, d, 0), 0.0)

    g = la_ref[...]
    b = g
    for d in range(1, steps):
        b = b + down(g, d, t_qk)
    last = jnp.where(t_qk == steps - 1, b, 0.0)
    b_last = last
    for d in range(1, steps):
        b_last = b_last + pltpu.roll(last, rows - d, 0)

    qa = qa_ref[...]
    ka = ka_ref[...]
    va = va_ref[...].astype(F32)
    qr = qr_ref[...]
    kr = kr_ref[...]
    vr = vr_ref[...].astype(F32)
    lg_lane = _head_lane_const(RET_LOG_DECAY, QK, DK)
    lg_v = _head_lane_const(RET_LOG_DECAY, VW, DV)

    oa = jnp.zeros((rows, VW), F32)
    orr = jnp.zeros((rows, VW), F32)
    for d in range(steps):
        ok = t_qk >= d
        w = jnp.where(ok, jnp.exp(jnp.where(ok, b - down(b, d, t_qk), 0.0)), 0.0)
        pa = (qa * down(ka, d, t_qk) * w).astype(BF16)
        oa = oa + _dot(pa, seg) * down(va, d, t_v)
        pr = (qr * down(kr, d, t_qk) * jnp.where(ok, jnp.exp(lg_lane * float(d)), 0.0)).astype(BF16)
        orr = orr + _dot(pr, seg) * down(vr, d, t_v)

    qta = (qa * jnp.exp(b)).astype(BF16)
    kda = (ka * jnp.exp(b_last - b)).astype(BF16)
    e_hi, e_mid, e_lo = _split3(jnp.exp(b_last))
    zero = jnp.zeros_like(e_hi)
    e_parts = jnp.where(t_qk == 0, e_hi, jnp.where(t_qk == 1, e_mid, jnp.where(t_qk == 2, e_lo, zero)))
    qtr = qr.astype(BF16)
    q_scale = jnp.exp(lg_v * (t_v.astype(F32) + 1.0))
    kdr = (kr * jnp.exp(lg_lane * (steps - 1.0 - t_qk.astype(F32)))).astype(BF16)
    ones = jnp.ones((8, DV), BF16)
    slab_row = lax.broadcasted_iota(jnp.int32, (8, 1), 0) // steps
    va_b = va_ref[...]
    vr_b = vr_ref[...]

    for p in range(n_seq // pair):
        slab = slice(8 * p, 8 * p + 8)
        for h in range(N_HEADS):
            ks = slice(h * DK, (h + 1) * DK)
            vs = slice(h * DV, (h + 1) * DV)
            inter_a = jnp.zeros((8, DV), F32)
            inter_r = jnp.zeros((8, DV), F32)
            for j in range(pair):
                s_idx = p * pair + j
                mine = slab_row == j
                s0a = sg0_ref[s_idx, h]
                s0r = sr0_ref[s_idx, h]
                inter_a = jnp.where(mine, _dot(qta[slab, ks], s0a.astype(BF16)), inter_a)
                inter_r = jnp.where(mine, _dot(qtr[slab, ks], s0r.astype(BF16)), inter_r)
                e_col = _dot_tn(jnp.where(mine, e_parts[slab, ks], 0.0).astype(BF16), ones)
                kd = jnp.where(mine, kda[slab, ks], 0.0).astype(BF16)
                sg_ref[s_idx, h] = e_col * s0a + _dot_tn(kd, va_b[slab, vs])
                kd = jnp.where(mine, kdr[slab, ks], 0.0).astype(BF16)
                sr_ref[s_idx, h] = math.exp(RET_LOG_DECAY[h] * steps) * s0r + _dot_tn(kd, vr_b[slab, vs])
            raw_ref[slab, vs] = oa[slab, vs] + inter_a
            raw_ref[slab, VW + h * DV:VW + (h + 1) * DV] = orr[slab, vs] + q_scale[slab, vs] * inter_r

    for h in range(N_HEADS):
        vs = slice(h * DV, (h + 1) * DV)
        cols = slice(VW + h * DV, VW + (h + 1) * DV)
        o_ref[:, vs] = _gated_norm(raw_ref[:, vs], gn_ref[:, vs], ra_ref[:, vs])
        o_ref[:, cols] = _gated_norm(raw_ref[:, cols], gn_ref[:, cols], gr_ref[:, vs])


def _sample_scan_call(qa, ka, la, va, qr, kr, vr, ra, gr, gn, sg0, sr0, steps):
    n_seq = sg0.shape[0]
    rows = SAMPLE_SEQS * steps
    row = lambda w: pl.BlockSpec((rows, w), lambda i: (i, 0))
    state = pl.BlockSpec((SAMPLE_SEQS, N_HEADS, DK, DV), lambda i: (i, 0, 0, 0))
    state_shape = jax.ShapeDtypeStruct(sg0.shape, F32)
    return pl.pallas_call(
        functools.partial(_sample_scan_kernel, steps=steps),
        grid=(n_seq // SAMPLE_SEQS,),
        in_specs=[row(QK), row(QK), row(QK), row(VW), row(QK), row(QK), row(VW), row(VW), row(VW), _whole(gn),
                  state, state],
        out_specs=[row(2 * VW), state, state],
        out_shape=[jax.ShapeDtypeStruct((n_seq * steps, 2 * VW), BF16), state_shape, state_shape],
        scratch_shapes=[pltpu.VMEM((rows, 2 * VW), F32)],
        compiler_params=pltpu.CompilerParams(
            dimension_semantics=("arbitrary",), vmem_limit_bytes=VMEM_LIMIT),
        name="sample_scan",
    )(qa, ka, la, va, qr, kr, vr, ra, gr, gn, sg0, sr0)


def _post_chain(rows, o_ref, h1_ref, gg_ref, p_ref,
                wout_ref, n2_ref, w2i_ref, w2o_ref, npl_ref, wpg_ref, wpp_ref, nf_ref, y_ref, final):
    d = h1_ref.shape[1]
    mix = (gg_ref[rows, :d].astype(F32) * _dot(o_ref[rows, :VW], wout_ref[:VW, :])
           + gg_ref[rows, d:].astype(F32) * _dot(o_ref[rows, VW:], wout_ref[VW:, :]))
    yield
    h = h1_ref[rows, :] + mix
    u = _rms(h, n2_ref[...]).astype(BF16)
    yield
    f = _ffn(u, w2i_ref, w2o_ref)
    yield
    h = h + 0.5 * f
    u = _rms(h, npl_ref[...]).astype(BF16)
    yield
    gate = _dot(u, wpg_ref[...])
    proj = _dot(p_ref[rows, :].astype(BF16), wpp_ref[...])
    yield
    h = h + proj * _sigmoid(gate)
    y_ref[rows, :] = _rms(h, nf_ref[...]) if final else h
    yield


_POST_ORDER = (0, 0, 1, 0, 1, 1, 0, 0, 1, 1, 0, 1)


def _post_kernel(*refs, final):
    tile = refs[0].shape[0]
    chains = [_post_chain(slice(lo, lo + tile // 2), *refs, final) for lo in (0, tile // 2)]
    for c in _POST_ORDER:
        next(chains[c])


def _post_call(o, h1, gg, p, weights, final):
    n, d = h1.shape
    tm = POST_TILE
    row = lambda w: pl.BlockSpec((tm, w), lambda i: (i, 0))
    return pl.pallas_call(
        functools.partial(_post_kernel, final=final),
        grid=(n // tm,),
        in_specs=[row(2 * VW), row(d), row(2 * d), row(p.shape[1])] + [_whole(w) for w in weights],
        out_specs=row(d),
        out_shape=jax.ShapeDtypeStruct((n, d), F32),
        compiler_params=pltpu.CompilerParams(
            dimension_semantics=("arbitrary",), vmem_limit_bytes=VMEM_LIMIT),
        name="post",
    )(o, h1, gg, p, *weights)


def _rotary_table(pos, rows):
    half = DK // 2
    freq = ROPE_BASE ** (-np.arange(half, dtype=np.float64) / half)
    ang = np.asarray(pos, np.float64)[:, None] * freq[None, :]
    cos, sin = np.cos(ang), np.sin(ang)
    reps = (rows // len(pos), N_HEADS)
    return jnp.asarray(np.concatenate([np.tile(np.concatenate([cos, cos], axis=-1), reps),
                                       np.tile(np.concatenate([-sin, sin], axis=-1), reps)], axis=-1), F32)


def kernel(x_prompt, x_sample, state_gla, state_ret, p_prompt, p_sample, norm_ffn1, w_ffn1_in, w_ffn1_out, norm_mix, w_in, w_alpha_up, b_alpha, gn_gla, gn_ret, w_out, norm_ffn2, w_ffn2_in, w_ffn2_out, norm_ple, w_ple_gate, w_ple_proj, norm_final):
    bp, tp, d = x_prompt.shape
    bs, ts, _ = x_sample.shape
    depth = w_in.shape[0]
    assert w_in.shape[2] == N_MAIN + LOW_RANK + 2 * d
    assert tp % PRE_TILE == 0 and tp % POST_TILE == 0 and tp % REF_CHUNK == 0 and 3 <= ts < REF_CHUNK and 8 % ts == 0
    assert (bs * ts) % POST_TILE == 0 and (bs * ts) % PRE_TILE == 0 and bs % SAMPLE_SEQS == 0

    table_p = _rotary_table(np.arange(tp), tp)
    table_s = _rotary_table(PAST_LEN + np.arange(ts), PRE_TILE)

    hp = x_prompt.reshape(bp * tp, d)
    hs = x_sample.reshape(bs * ts, d)
    row = lambda v: v.reshape(1, -1).astype(F32)
    gla_p, ret_p, gla_s, ret_s = [], [], [], []
    for i in range(depth):
        pre_w = (row(norm_ffn1[i]), w_ffn1_in[i].astype(BF16), w_ffn1_out[i].astype(BF16), row(norm_mix[i]),
                 w_in[i].astype(BF16), w_alpha_up[i].astype(BF16), row(b_alpha[i]))
        post_cast = (w_out[i], w_ffn2_in[i], w_ffn2_out[i], w_ple_gate[i], w_ple_proj[i])
        final = i == depth - 1

        gn = row(jnp.concatenate([gn_gla[i], gn_ret[i]]))
        (h1, gg, o, sg, sr), post_bf = _front_call(hp, table_p, (*pre_w, gn), post_cast, bp, tp)
        wout_b, w2i_b, w2o_b, wpg_b, wpp_b = post_bf
        post_w = (wout_b, row(norm_ffn2[i]), w2i_b, w2o_b, row(norm_ple[i]), wpg_b, wpp_b, row(norm_final))
        hp = _post_call(o, h1, gg, p_prompt[i].reshape(bp * tp, -1), post_w, final)
        gla_p.append(sg.astype(state_gla.dtype))
        ret_p.append(sr.astype(state_ret.dtype))

        h1, gg, qa, ka, va, la, qr, kr, vr, ra, gr = _pre_call(hs, table_s, pre_w)
        o, sg, sr = _sample_scan_call(qa, ka, la, va, qr, kr, vr, ra, gr, gn,
                                      state_gla[i].astype(F32), state_ret[i].astype(F32), ts)
        hs = _post_call(o, h1, gg, p_sample[i].reshape(bs * ts, -1), post_w, final)
        gla_s.append(sg.astype(state_gla.dtype))
        ret_s.append(sr.astype(state_ret.dtype))

    return (hp.reshape(bp, tp, d), hs.reshape(bs, ts, d),
            jnp.stack(gla_p), jnp.stack(ret_p), jnp.stack(gla_s), jnp.stack(ret_s))
```

```python
import functools
import math

import numpy as np

import jax
import jax.numpy as jnp
from jax import lax
from jax.experimental import pallas as pl
from jax.experimental.pallas import tpu as pltpu

F32 = jnp.float32
BF16 = jnp.bfloat16

N_HEADS = 4
DK = 64
DV = 128
QK = N_HEADS * DK
VW = N_HEADS * DV
N_MAIN = 4 * QK + 4 * VW
LOW_RANK = 16
GATE_TAU = 16.0
ROPE_BASE = 10000.0
PAST_LEN = 16384
REF_CHUNK = 64
EPS = 1e-6
RET_LOG_DECAY = tuple(math.log1p(-(2.0 ** (-5.0 - h))) for h in range(N_HEADS))

PRE_TILE = 256
POST_TILE = 512
SCAN_CHUNK = 256
SAMPLE_SEQS = 16
BF16_ROWS = 16
MXU_TILE = 256
VMEM_LIMIT = 56 * 1024 * 1024
GLA_SAFE_LOG_DECAY = -60.0

NT = (((1,), (1,)), ((), ()))
TN = (((0,), (0,)), ((), ()))


def _dot(a, b):
    return jnp.dot(a, b, preferred_element_type=F32)


def _dot_nt(a, b):
    return lax.dot_general(a, b, NT, preferred_element_type=F32)


def _dot_tn(a, b):
    return lax.dot_general(a, b, TN, preferred_element_type=F32)


def _rms(x, w):
    return x * lax.rsqrt(jnp.mean(x * x, axis=-1, keepdims=True) + EPS) * w


def _sigmoid(x):
    return 0.5 * jnp.tanh(0.5 * x) + 0.5


def _ffn(u, wi_ref, wo_ref):
    d_ff = wo_ref.shape[0]
    cut = -(-d_ff // (2 * MXU_TILE)) * MXU_TILE
    out = None
    for lo, hi in ((0, cut), (cut, d_ff)):
        a = _dot(u, wi_ref[:, lo:hi])
        b = _dot(u, wi_ref[:, d_ff + lo:d_ff + hi])
        g = (a * _sigmoid(a) * b).astype(BF16)
        part = _dot(g, wo_ref[lo:hi, :])
        out = part if out is None else out + part
    return out


def _split3(x):
    hi = x.astype(BF16)
    r1 = x - hi.astype(F32)
    mid = r1.astype(BF16)
    lo = (r1 - mid.astype(F32)).astype(BF16)
    return hi, mid, lo


def _head_lane_const(values, width, per_head):
    lane = lax.broadcasted_iota(jnp.int32, (1, width), 1) // per_head
    out = jnp.full((1, width), values[-1], F32)
    for h in range(len(values) - 2, -1, -1):
        out = jnp.where(lane == h, values[h], out)
    return out


def _head_segment_sum():
    lane_head = lax.broadcasted_iota(jnp.int32, (QK, VW), 0) // DK
    col_head = lax.broadcasted_iota(jnp.int32, (QK, VW), 1) // DV
    return jnp.where(lane_head == col_head, 1.0, 0.0).astype(BF16)


def _swap_halves(x):
    n = x.shape[-1]
    lane = lax.broadcasted_iota(jnp.int32, x.shape, 1)
    fwd = pltpu.roll(x, n - DK // 2, 1)
    bwd = pltpu.roll(x, DK // 2, 1)
    return jnp.where(lane % DK < DK // 2, fwd, bwd)


def _gated_norm(o, w, gate):
    return (o * lax.rsqrt(jnp.mean(o * o, axis=-1, keepdims=True) + EPS) * w * gate.astype(F32)).astype(BF16)


def _whole(_):
    return pl.BlockSpec(memory_space=pltpu.VMEM)


def _pre_chain(rows, x_ref, tab_ref, n1_ref, w1i_ref, w1o_ref, nm_ref, win_ref, wup_ref, balpha_ref,
               wgate_ref, h1_ref, gg_ref,
               qa_dst, ka_dst, va_dst, la_dst, qr_dst, kr_dst, vr_dst, ra_dst, gr_dst, decay_sums):
    x = x_ref[rows, :]
    u = _rms(x, n1_ref[...]).astype(BF16)
    yield
    f = _ffn(u, w1i_ref, w1o_ref)
    yield
    h = x + 0.5 * f
    h1_ref[rows, :] = h
    u = _rms(h, nm_ref[...]).astype(BF16)
    yield

    main = _dot(u, win_ref[:, :N_MAIN])

    def proj(lo, hi):
        return main[:, lo:hi]

    o = 0
    qa_dst[rows, :] = proj(o, o + QK) * (DK ** -0.5); o += QK
    ka_dst[rows, :] = proj(o, o + QK); o += QK
    va_dst[rows, :] = proj(o, o + VW).astype(BF16); o += VW
    ra = proj(o, o + VW); o += VW
    ra_dst[rows, :] = (ra * _sigmoid(ra)).astype(BF16)
    cos = tab_ref[rows, :QK]
    sin = tab_ref[rows, QK:]
    qr = proj(o, o + QK); o += QK
    qr_dst[rows, :] = qr * cos + _swap_halves(qr) * sin
    kr = proj(o, o + QK); o += QK
    kr_dst[rows, :] = (kr * cos + _swap_halves(kr) * sin) * (DK ** -0.5)
    vr_dst[rows, :] = proj(o, o + VW).astype(BF16); o += VW
    gr = proj(o, o + VW); o += VW
    gr_dst[rows, :] = (gr * _sigmoid(gr)).astype(BF16)
    yield
    a_low = _dot(u, win_ref[:, N_MAIN:N_MAIN + LOW_RANK])
    z = _dot(a_low.astype(BF16), wup_ref[...]) + balpha_ref[...]
    log_sig = jnp.minimum(z, 0.0) - jnp.log1p(jnp.exp(-jnp.abs(z)))
    la = log_sig * (1.0 / GATE_TAU)
    la_dst[rows, :] = la
    for lo in range(0, la.shape[0], SCAN_CHUNK):
        decay_sums.append(jnp.min(jnp.sum(la[lo:lo + SCAN_CHUNK, :], axis=0, keepdims=True)))
    gg_ref[rows, :] = _sigmoid(_dot(u, wgate_ref[...])).astype(BF16)
    yield


_PRE_ORDER = (0, 0, 1, 0, 1, 0, 1, 0, 1, 1)


def _pre_kernel(x_ref, tab_ref, n1_ref, w1i_ref, w1o_ref, nm_ref, win_ref, wup_ref, balpha_ref,
                h1_ref, gg_ref, qa_ref, ka_ref, va_ref, la_ref, qr_ref, kr_ref, vr_ref, ra_ref, gr_ref,
                wgate_ref):
    @pl.when(pl.program_id(0) == 0)
    def _():
        wgate_ref[...] = win_ref[:, N_MAIN + LOW_RANK:]

    tile = x_ref.shape[0]
    chains = [_pre_chain(slice(lo, lo + tile // 2), x_ref, tab_ref, n1_ref, w1i_ref, w1o_ref, nm_ref,
                         win_ref, wup_ref, balpha_ref, wgate_ref, h1_ref, gg_ref,
                         qa_ref, ka_ref, va_ref, la_ref, qr_ref, kr_ref, vr_ref, ra_ref, gr_ref, [])
              for lo in (0, tile // 2)]
    for c in _PRE_ORDER:
        next(chains[c])


def _pre_call(x, table, weights):
    n, d = x.shape
    tm = PRE_TILE
    table_blocks = table.shape[0] // tm
    row = lambda w: pl.BlockSpec((tm, w), lambda i: (i, 0))
    tab = pl.BlockSpec((tm, 2 * QK), lambda i: (i % table_blocks, 0))
    out_widths = [(d, F32), (2 * d, BF16), (QK, F32), (QK, F32), (VW, BF16), (QK, F32), (QK, F32), (QK, F32),
                  (VW, BF16), (VW, BF16), (VW, BF16)]
    return pl.pallas_call(
        _pre_kernel,
        grid=(n // tm,),
        in_specs=[row(d), tab] + [_whole(w) for w in weights],
        out_specs=[row(w) for w, _ in out_widths],
        out_shape=[jax.ShapeDtypeStruct((n, w), t) for w, t in out_widths],
        scratch_shapes=[pltpu.VMEM((d, 2 * d), BF16)],
        compiler_params=pltpu.CompilerParams(
            dimension_semantics=("arbitrary",), vmem_limit_bytes=VMEM_LIMIT),
        name="pre",
    )(x, table, *weights)


def _front_kernel(x_ref, tab_ref, n1_ref, w1i_ref, w1o_ref, nm_ref, win_ref, wup_ref, balpha_ref, gn_ref,
                  *rest, n_cast, tiles_per_seq):
    cast_in = rest[:n_cast]
    h1_ref, gg_ref, o_ref, sg_ref, sr_ref = rest[n_cast:n_cast + 5]
    cast_out = rest[n_cast + 5:2 * n_cast + 5]
    wgate_ref = rest[2 * n_cast + 5]
    handoff = rest[2 * n_cast + 6:2 * n_cast + 15]
    stg_ref, str_ref, b_ref, acc_ref, ks_ref, bs_ref, vs_ref, decay_ref = rest[2 * n_cast + 15:]
    i = pl.program_id(0)
    slot = i % 2
    written = [ref.at[slot] for ref in handoff]
    qa_s, ka_s, va_s, la_s, qr_s, kr_s, vr_s, ra_s, gr_s = (ref.at[1 - slot] for ref in handoff)
    c_len = SCAN_CHUNK
    tile = x_ref.shape[0]
    n_chunks = tile // c_len
    r = lax.broadcasted_iota(jnp.int32, (c_len, c_len), 0)
    c = lax.broadcasted_iota(jnp.int32, (c_len, c_len), 1)
    causal = c <= r

    def chunk_cumsum(la):
        tril = jnp.where(causal, 1.0, 0.0).astype(BF16)
        hi, mid, lo = _split3(la)
        return _dot(tril, hi) + _dot(tril, mid) + _dot(tril, lo)

    @pl.when(i == 0)
    def _():
        wgate_ref[...] = win_ref[:, N_MAIN + LOW_RANK:]
        for ref in (*handoff, acc_ref):
            ref[...] = jnp.zeros_like(ref)
        decay_ref[0] = 0.0

    @pl.when((i == 0) | (i % tiles_per_seq == 1))
    def _():
        stg_ref[...] = jnp.zeros_like(stg_ref)
        str_ref[...] = jnp.zeros_like(str_ref)

    @pl.when(decay_ref[0] < GLA_SAFE_LOG_DECAY)
    def _():
        seg = _head_segment_sum()
        in_chunk = lax.broadcasted_iota(jnp.int32, (tile, QK), 0) % c_len
        for n in range(n_chunks):
            rows = slice(n * c_len, (n + 1) * c_len)
            b_ref[rows, :] = chunk_cumsum(la_s[rows, :])
        ks_ref[...] = ka_s[...]
        bs_ref[...] = b_ref[...]
        vs_ref[...] = va_s[...].astype(F32)
        acc_ref[...] = jnp.zeros_like(acc_ref)

        def body(d, carry):
            ok = in_chunk >= d
            k_sh, b_sh, v_sh = ks_ref[...], bs_ref[...], vs_ref[...]
            w = jnp.exp(jnp.where(ok, b_ref[...] - b_sh, 0.0))
            p = jnp.where(ok, qa_s[...] * k_sh * w, 0.0).astype(BF16)
            acc_ref[...] += _dot(p, seg) * v_sh
            ks_ref[...] = pltpu.roll(k_sh, 1, 0)
            bs_ref[...] = pltpu.roll(b_sh, 1, 0)
            vs_ref[...] = pltpu.roll(v_sh, 1, 0)
            return carry

        lax.fori_loop(0, c_len, body, 0)

    for src, dst in zip(cast_in, cast_out):
        dst[...] = src[...].astype(BF16)

    def scan():
        exact = decay_ref[0] < GLA_SAFE_LOG_DECAY
        rr = lax.broadcasted_iota(jnp.int32, (tile, tile), 0)
        cc = lax.broadcasted_iota(jnp.int32, (tile, tile), 1)
        ret_causal = cc <= rr
        dist = (rr - cc).astype(F32)
        lg_lane = _head_lane_const(RET_LOG_DECAY, QK, DK)
        row_qk = lax.broadcasted_iota(jnp.int32, (tile, QK), 0).astype(F32)
        k_decay = jnp.exp(lg_lane * (tile - 1.0 - row_qk))
        q_decay = jnp.exp(lg_lane * (row_qk + 1.0))
        tile_decay = jnp.exp(lg_lane * float(tile))
        gla = []
        for n in range(n_chunks):
            rows = slice(n * c_len, (n + 1) * c_len)
            b = chunk_cumsum(la_s[rows, :])
            b_last = b[c_len - 1:c_len, :]
            k = ka_s[rows, :]
            qt = (qa_s[rows, :] * jnp.exp(b)).astype(BF16)
            kt = (k * jnp.exp(jnp.minimum(-b, -GLA_SAFE_LOG_DECAY))).astype(BF16)
            kd = (k * jnp.exp(b_last - b)).astype(BF16)
            gla.append((qt, kt, kd, jnp.exp(b_last), va_s[rows, :]))
        kf = kr_s[...]
        qf = qr_s[...]
        rq, rqd = qf.astype(BF16), (qf * q_decay).astype(BF16)
        rk, rkd = kf.astype(BF16), (kf * k_decay).astype(BF16)
        rv = vr_s[...]
        yield
        for n in range(n_chunks):
            rows = slice(n * c_len, (n + 1) * c_len)
            qt, kt, kd, e_last, v_all = gla[n]
            st = stg_ref[...]
            st_b = st.astype(BF16)
            stg_ref[...] = st * e_last
            for h in range(N_HEADS):
                ks = slice(h * DK, (h + 1) * DK)
                vs = slice(h * DV, (h + 1) * DV)
                v = v_all[:, vs]
                s = jnp.where(causal, _dot_nt(qt[:, ks], kt[:, ks]), 0.0).astype(BF16)
                intra = jnp.where(exact, acc_ref[rows, vs], _dot(s, v))
                o_ref[rows, vs] = _gated_norm(intra + _dot_nt(qt[:, ks], st_b[:, ks]), gn_ref[:, vs], ra_s[rows, vs])
                stg_ref[:, ks] += _dot_tn(v, kd[:, ks])
            yield
        st = str_ref[...]
        st_b = st.astype(BF16)
        str_ref[...] = st * tile_decay
        for h in range(N_HEADS):
            ks = slice(h * DK, (h + 1) * DK)
            vs = slice(h * DV, (h + 1) * DV)
            v = rv[:, vs]
            lg = RET_LOG_DECAY[h]
            s = (_dot_nt(rq[:, ks], rk[:, ks]) * jnp.where(ret_causal, jnp.exp(lg * dist), 0.0)).astype(BF16)
            cols = slice(VW + h * DV, VW + (h + 1) * DV)
            o_ref[:, cols] = _gated_norm(_dot(s, v) + _dot_nt(rqd[:, ks], st_b[:, ks]),
                                         gn_ref[:, cols], gr_s[:, vs])
            str_ref[:, ks] += _dot_tn(v, rkd[:, ks])
        yield

    last = pl.num_programs(0) - 1

    @pl.when(i < last)
    def _():
        decay_sums = []
        chains = [_pre_chain(slice(lo, lo + tile // FRONT_PRE_CHAINS), x_ref, tab_ref, n1_ref, w1i_ref, w1o_ref,
                             nm_ref, win_ref, wup_ref, balpha_ref, wgate_ref, h1_ref, gg_ref, *written, decay_sums)
                  for lo in range(0, tile, tile // FRONT_PRE_CHAINS)]
        chains.append(scan())
        for ch in _FRONT_ORDER:
            next(chains[ch])
        decay_ref[0] = functools.reduce(jnp.minimum, decay_sums)

    @pl.when(i == last)
    def _():
        for _ in scan():
            pass

    @pl.when((i > 0) & (i % tiles_per_seq == 0))
    def _():
        for h in range(N_HEADS):
            ks = slice(h * DK, (h + 1) * DK)
            sg_ref[0, h] = stg_ref[:, ks].T
            sr_ref[0, h] = str_ref[:, ks].T


FRONT_PRE_CHAINS = 1
_FRONT_ORDER = (1, 0, 0, 1, 0, 1, 0, 0)


def _cast_blocks(rows, steps):
    return max(n for n in range(1, steps + 1) if rows % n == 0 and (rows // n) % BF16_ROWS == 0)


def _front_call(x, table, weights, to_cast, batch, seq):
    n, d = x.shape
    tm = PRE_TILE
    assert tm % SCAN_CHUNK == 0 and seq % tm == 0
    n_tiles = n // tm
    tiles_per_seq = seq // tm
    table_blocks = table.shape[0] // tm
    cur = lambda i: jnp.minimum(i, n_tiles - 1)
    prev = lambda i: jnp.maximum(i - 1, 0)
    row = lambda w: pl.BlockSpec((tm, w), lambda i: (cur(i), 0))
    tab = pl.BlockSpec((tm, 2 * QK), lambda i: (cur(i) % table_blocks, 0))
    state = pl.BlockSpec((1, N_HEADS, DK, DV), lambda i: (prev(i) // tiles_per_seq, 0, 0, 0))
    state_shape = jax.ShapeDtypeStruct((batch, N_HEADS, DK, DV), F32)
    out_widths = [(d, F32), (2 * d, BF16)]

    def slab(w):
        blocks = _cast_blocks(w.shape[0], n_tiles)
        return pl.BlockSpec((w.shape[0] // blocks, w.shape[1]), lambda i: (jnp.minimum(i, blocks - 1), 0))

    cast_specs = [slab(w) for w in to_cast]
    vm = lambda rows, w, t: pltpu.VMEM((rows, w), t)
    two = lambda w, t: pltpu.VMEM((2, tm, w), t)
    outs = pl.pallas_call(
        functools.partial(_front_kernel, n_cast=len(to_cast), tiles_per_seq=tiles_per_seq),
        grid=(n_tiles + 1,),
        in_specs=[row(d), tab] + [_whole(w) for w in weights] + cast_specs,
        out_specs=[row(w) for w, _ in out_widths]
        + [pl.BlockSpec((tm, 2 * VW), lambda i: (prev(i), 0)), state, state] + cast_specs,
        out_shape=[jax.ShapeDtypeStruct((n, w), t) for w, t in out_widths]
        + [jax.ShapeDtypeStruct((n, 2 * VW), BF16), state_shape, state_shape]
        + [jax.ShapeDtypeStruct(w.shape, BF16) for w in to_cast],
        scratch_shapes=[vm(d, 2 * d, BF16),
                        two(QK, F32), two(QK, F32), two(VW, BF16), two(QK, F32),
                        two(QK, F32), two(QK, F32), two(VW, BF16), two(VW, BF16), two(VW, BF16),
                        vm(DV, QK, F32), vm(DV, QK, F32),
                        vm(tm, QK, F32), vm(tm, VW, F32), vm(tm, QK, F32), vm(tm, QK, F32), vm(tm, VW, F32),
                        pltpu.SMEM((1,), F32)],
        compiler_params=pltpu.CompilerParams(
            dimension_semantics=("arbitrary",), vmem_limit_bytes=VMEM_LIMIT),
        name="front",
    )(x, table, *weights, *to_cast)
    return outs[:5], outs[5:]


def _sample_scan_kernel(qa_ref, ka_ref, la_ref, va_ref, qr_ref, kr_ref, vr_ref, ra_ref, gr_ref, gn_ref,
                        sg0_ref, sr0_ref, o_ref, sg_ref, sr_ref, raw_ref, *, steps):
    rows = qa_ref.shape[0]
    n_seq = rows // steps
    pair = 8 // steps
    t_qk = lax.broadcasted_iota(jnp.int32, (rows, QK), 0) % steps
    t_v = lax.broadcasted_iota(jnp.int32, (rows, VW), 0) % steps
    seg = _head_segment_sum()

    def down(x, d, t):
        return x if d == 0 else jnp.where(t >= d, pltpu.roll(x, d, 0), 0.0)

    g = la_ref[...]
    b = g
    for d in range(1, steps):
        b = b + down(g, d, t_qk)
    last = jnp.where(t_qk == steps - 1, b, 0.0)
    b_last = last
    for d in range(1, steps):
        b_last = b_last + pltpu.roll(last, rows - d, 0)

    qa = qa_ref[...]
    ka = ka_ref[...]
    va = va_ref[...].astype(F32)
    qr = qr_ref[...]
    kr = kr_ref[...]
    vr = vr_ref[...].astype(F32)
    lg_lane = _head_lane_const(RET_LOG_DECAY, QK, DK)
    lg_v = _head_lane_const(RET_LOG_DECAY, VW, DV)

    oa = jnp.zeros((rows, VW), F32)
    orr = jnp.zeros((rows, VW), F32)
    for d in range(steps):
        ok = t_qk >= d
        w = jnp.where(ok, jnp.exp(jnp.where(ok, b - down(b, d, t_qk), 0.0)), 0.0)
        pa = (qa * down(ka, d, t_qk) * w).astype(BF16)
        oa = oa + _dot(pa, seg) * down(va, d, t_v)
        pr = (qr * down(kr, d, t_qk) * jnp.where(ok, jnp.exp(lg_lane * float(d)), 0.0)).astype(BF16)
        orr = orr + _dot(pr, seg) * down(vr, d, t_v)

    qta = (qa * jnp.exp(b)).astype(BF16)
    kda = (ka * jnp.exp(b_last - b)).astype(BF16)
    e_hi, e_mid, e_lo = _split3(jnp.exp(b_last))
    zero = jnp.zeros_like(e_hi)
    e_parts = jnp.where(t_qk == 0, e_hi, jnp.where(t_qk == 1, e_mid, jnp.where(t_qk == 2, e_lo, zero)))
    qtr = qr.astype(BF16)
    q_scale = jnp.exp(lg_v * (t_v.astype(F32) + 1.0))
    kdr = (kr * jnp.exp(lg_lane * (steps - 1.0 - t_qk.astype(F32)))).astype(BF16)
    ones = jnp.ones((8, DV), BF16)
    slab_row = lax.broadcasted_iota(jnp.int32, (8, 1), 0) // steps
    va_b = va_ref[...]
    vr_b = vr_ref[...]

    for p in range(n_seq // pair):
        slab = slice(8 * p, 8 * p + 8)
        for h in range(N_HEADS):
            ks = slice(h * DK, (h + 1) * DK)
            vs = slice(h * DV, (h + 1) * DV)
            inter_a = jnp.zeros((8, DV), F32)
            inter_r = jnp.zeros((8, DV), F32)
            for j in range(pair):
                s_idx = p * pair + j
                mine = slab_row == j
                s0a = sg0_ref[s_idx, h]
                s0r = sr0_ref[s_idx, h]
                inter_a = jnp.where(mine, _dot(qta[slab, ks], s0a.astype(BF16)), inter_a)
                inter_r = jnp.where(mine, _dot(qtr[slab, ks], s0r.astype(BF16)), inter_r)
                e_col = _dot_tn(jnp.where(mine, e_parts[slab, ks], 0.0).astype(BF16), ones)
                kd = jnp.where(mine, kda[slab, ks], 0.0).astype(BF16)
                sg_ref[s_idx, h] = e_col * s0a + _dot_tn(kd, va_b[slab, vs])
                kd = jnp.where(mine, kdr[slab, ks], 0.0).astype(BF16)
                sr_ref[s_idx, h] = math.exp(RET_LOG_DECAY[h] * steps) * s0r + _dot_tn(kd, vr_b[slab, vs])
            raw_ref[slab, vs] = oa[slab, vs] + inter_a
            raw_ref[slab, VW + h * DV:VW + (h + 1) * DV] = orr[slab, vs] + q_scale[slab, vs] * inter_r

    for h in range(N_HEADS):
        vs = slice(h * DV, (h + 1) * DV)
        cols = slice(VW + h * DV, VW + (h + 1) * DV)
        o_ref[:, vs] = _gated_norm(raw_ref[:, vs], gn_ref[:, vs], ra_ref[:, vs])
        o_ref[:, cols] = _gated_norm(raw_ref[:, cols], gn_ref[:, cols], gr_ref[:, vs])


def _sample_scan_call(qa, ka, la, va, qr, kr, vr, ra, gr, gn, sg0, sr0, steps):
    n_seq = sg0.shape[0]
    rows = SAMPLE_SEQS * steps
    row = lambda w: pl.BlockSpec((rows, w), lambda i: (i, 0))
    state = pl.BlockSpec((SAMPLE_SEQS, N_HEADS, DK, DV), lambda i: (i, 0, 0, 0))
    state_shape = jax.ShapeDtypeStruct(sg0.shape, F32)
    return pl.pallas_call(
        functools.partial(_sample_scan_kernel, steps=steps),
        grid=(n_seq // SAMPLE_SEQS,),
        in_specs=[row(QK), row(QK), row(QK), row(VW), row(QK), row(QK), row(VW), row(VW), row(VW), _whole(gn),
                  state, state],
        out_specs=[row(2 * VW), state, state],
        out_shape=[jax.ShapeDtypeStruct((n_seq * steps, 2 * VW), BF16), state_shape, state_shape],
        scratch_shapes=[pltpu.VMEM((rows, 2 * VW), F32)],
        compiler_params=pltpu.CompilerParams(
            dimension_semantics=("arbitrary",), vmem_limit_bytes=VMEM_LIMIT),
        name="sample_scan",
    )(qa, ka, la, va, qr, kr, vr, ra, gr, gn, sg0, sr0)


def _post_chain(rows, o_ref, h1_ref, gg_ref, p_ref,
                wout_ref, n2_ref, w2i_ref, w2o_ref, npl_ref, wpg_ref, wpp_ref, nf_ref, y_ref, final):
    d = h1_ref.shape[1]
    mix = (gg_ref[rows, :d].astype(F32) * _dot(o_ref[rows, :VW], wout_ref[:VW, :])
           + gg_ref[rows, d:].astype(F32) * _dot(o_ref[rows, VW:], wout_ref[VW:, :]))
    yield
    h = h1_ref[rows, :] + mix
    u = _rms(h, n2_ref[...]).astype(BF16)
    yield
    f = _ffn(u, w2i_ref, w2o_ref)
    yield
    h = h + 0.5 * f
    u = _rms(h, npl_ref[...]).astype(BF16)
    yield
    gate = _dot(u, wpg_ref[...])
    proj = _dot(p_ref[rows, :].astype(BF16), wpp_ref[...])
    yield
    h = h + proj * _sigmoid(gate)
    y_ref[rows, :] = _rms(h, nf_ref[...]) if final else h
    yield


_POST_ORDER = (0, 0, 1, 0, 1, 1, 0, 0, 1, 1, 0, 1)


def _post_kernel(*refs, final):
    tile = refs[0].shape[0]
    chains = [_post_chain(slice(lo, lo + tile // 2), *refs, final) for lo in (0, tile // 2)]
    for c in _POST_ORDER:
        next(chains[c])


def _post_call(o, h1, gg, p, weights, final):
    n, d = h1.shape
    tm = POST_TILE
    row = lambda w: pl.BlockSpec((tm, w), lambda i: (i, 0))
    return pl.pallas_call(
        functools.partial(_post_kernel, final=final),
        grid=(n // tm,),
        in_specs=[row(2 * VW), row(d), row(2 * d), row(p.shape[1])] + [_whole(w) for w in weights],
        out_specs=row(d),
        out_shape=jax.ShapeDtypeStruct((n, d), F32),
        compiler_params=pltpu.CompilerParams(
            dimension_semantics=("arbitrary",), vmem_limit_bytes=VMEM_LIMIT),
        name="post",
    )(o, h1, gg, p, *weights)


def _rotary_table(pos, rows):
    half = DK // 2
    freq = ROPE_BASE ** (-np.arange(half, dtype=np.float64) / half)
    ang = np.asarray(pos, np.float64)[:, None] * freq[None, :]
    cos, sin = np.cos(ang), np.sin(ang)
    reps = (rows // len(pos), N_HEADS)
    return jnp.asarray(np.concatenate([np.tile(np.concatenate([cos, cos], axis=-1), reps),
                                       np.tile(np.concatenate([-sin, sin], axis=-1), reps)], axis=-1), F32)


def kernel(x_prompt, x_sample, state_gla, state_ret, p_prompt, p_sample, norm_ffn1, w_ffn1_in, w_ffn1_out, norm_mix, w_in, w_alpha_up, b_alpha, gn_gla, gn_ret, w_out, norm_ffn2, w_ffn2_in, w_ffn2_out, norm_ple, w_ple_gate, w_ple_proj, norm_final):
    bp, tp, d = x_prompt.shape
    bs, ts, _ = x_sample.shape
    depth = w_in.shape[0]
    assert w_in.shape[2] == N_MAIN + LOW_RANK + 2 * d
    assert tp % PRE_TILE == 0 and tp % POST_TILE == 0 and tp % REF_CHUNK == 0 and 3 <= ts < REF_CHUNK and 8 % ts == 0
    assert (bs * ts) % POST_TILE == 0 and (bs * ts) % PRE_TILE == 0 and bs % SAMPLE_SEQS == 0

    table_p = _rotary_table(np.arange(tp), tp)
    table_s = _rotary_table(PAST_LEN + np.arange(ts), PRE_TILE)

    hp = x_prompt.reshape(bp * tp, d)
    hs = x_sample.reshape(bs * ts, d)
    row = lambda v: v.reshape(1, -1).astype(F32)
    gla_p, ret_p, gla_s, ret_s = [], [], [], []
    for i in range(depth):
        pre_w = (row(norm_ffn1[i]), w_ffn1_in[i].astype(BF16), w_ffn1_out[i].astype(BF16), row(norm_mix[i]),
                 w_in[i].astype(BF16), w_alpha_up[i].astype(BF16), row(b_alpha[i]))
        post_cast = (w_out[i], w_ffn2_in[i], w_ffn2_out[i], w_ple_gate[i], w_ple_proj[i])
        final = i == depth - 1

        gn = row(jnp.concatenate([gn_gla[i], gn_ret[i]]))
        (h1, gg, o, sg, sr), post_bf = _front_call(hp, table_p, (*pre_w, gn), post_cast, bp, tp)
        wout_b, w2i_b, w2o_b, wpg_b, wpp_b = post_bf
        post_w = (wout_b, row(norm_ffn2[i]), w2i_b, w2o_b, row(norm_ple[i]), wpg_b, wpp_b, row(norm_final))
        hp = _post_call(o, h1, gg, p_prompt[i].reshape(bp * tp, -1), post_w, final)
        gla_p.append(sg.astype(state_gla.dtype))
        ret_p.append(sr.astype(state_ret.dtype))

        h1, gg, qa, ka, va, la, qr, kr, vr, ra, gr = _pre_call(hs, table_s, pre_w)
        o, sg, sr = _sample_scan_call(qa, ka, la, va, qr, kr, vr, ra, gr, gn,
                                      state_gla[i].astype(F32), state_ret[i].astype(F32), ts)
        hs = _post_call(o, h1, gg, p_sample[i].reshape(bs * ts, -1), post_w, final)
        gla_s.append(sg.astype(state_gla.dtype))
        ret_s.append(sr.astype(state_ret.dtype))

    return (hp.reshape(bp, tp, d), hs.reshape(bs, ts, d),
            jnp.stack(gla_p), jnp.stack(ret_p), jnp.stack(gla_s), jnp.stack(ret_s))
```

```python
import functools
import math

import numpy as np

import jax
import jax.numpy as jnp
from jax import lax
from jax.experimental import pallas as pl
from jax.experimental.pallas import tpu as pltpu

F32 = jnp.float32
BF16 = jnp.bfloat16

N_HEADS = 4
DK = 64
DV = 128
QK = N_HEADS * DK
VW = N_HEADS * DV
N_MAIN = 4 * QK + 4 * VW
LOW_RANK = 16
GATE_TAU = 16.0
ROPE_BASE = 10000.0
PAST_LEN = 16384
REF_CHUNK = 64
EPS = 1e-6
RET_LOG_DECAY = tuple(math.log1p(-(2.0 ** (-5.0 - h))) for h in range(N_HEADS))

PRE_TILE = 256
POST_TILE = 512
SCAN_CHUNK = 256
SAMPLE_SEQS = 32
BF16_ROWS = 16
MXU_TILE = 256
VMEM_LIMIT = 56 * 1024 * 1024
GLA_SAFE_LOG_DECAY = -60.0

NT = (((1,), (1,)), ((), ()))
TN = (((0,), (0,)), ((), ()))


def _dot(a, b):
    return jnp.dot(a, b, preferred_element_type=F32)


def _dot_nt(a, b):
    return lax.dot_general(a, b, NT, preferred_element_type=F32)


def _dot_tn(a, b):
    return lax.dot_general(a, b, TN, preferred_element_type=F32)


def _rms(x, w):
    return x * lax.rsqrt(jnp.mean(x * x, axis=-1, keepdims=True) + EPS) * w


def _sigmoid(x):
    return jax.nn.sigmoid(x)


def _ffn(u, wi_ref, wo_ref):
    d_ff = wo_ref.shape[0]
    cut = -(-d_ff // (2 * MXU_TILE)) * MXU_TILE
    out = None
    for lo, hi in ((0, cut), (cut, d_ff)):
        a = _dot(u, wi_ref[:, lo:hi])
        b = _dot(u, wi_ref[:, d_ff + lo:d_ff + hi])
        g = (a * _sigmoid(a) * b).astype(BF16)
        part = _dot(g, wo_ref[lo:hi, :])
        out = part if out is None else out + part
    return out


def _split3(x):
    hi = x.astype(BF16)
    r1 = x - hi.astype(F32)
    mid = r1.astype(BF16)
    lo = (r1 - mid.astype(F32)).astype(BF16)
    return hi, mid, lo


def _head_lane_const(values, width, per_head):
    lane = lax.broadcasted_iota(jnp.int32, (1, width), 1) // per_head
    out = jnp.full((1, width), values[-1], F32)
    for h in range(len(values) - 2, -1, -1):
        out = jnp.where(lane == h, values[h], out)
    return out


def _head_segment_sum():
    lane_head = lax.broadcasted_iota(jnp.int32, (QK, VW), 0) // DK
    col_head = lax.broadcasted_iota(jnp.int32, (QK, VW), 1) // DV
    return jnp.where(lane_head == col_head, 1.0, 0.0).astype(BF16)


def _swap_halves(x):
    n = x.shape[-1]
    lane = lax.broadcasted_iota(jnp.int32, x.shape, 1)
    fwd = pltpu.roll(x, n - DK // 2, 1)
    bwd = pltpu.roll(x, DK // 2, 1)
    return jnp.where(lane % DK < DK // 2, fwd, bwd)


def _gated_norm(o, w, gate):
    return (o * lax.rsqrt(jnp.mean(o * o, axis=-1, keepdims=True) + EPS) * w * gate.astype(F32)).astype(BF16)


def _whole(_):
    return pl.BlockSpec(memory_space=pltpu.VMEM)


def _pre_chain(rows, x_ref, tab_ref, n1_ref, w1i_ref, w1o_ref, nm_ref, win_ref, wup_ref, balpha_ref,
               wgate_ref, h1_ref, gg_ref,
               qa_dst, ka_dst, va_dst, la_dst, qr_dst, kr_dst, vr_dst, ra_dst, gr_dst, decay_sums):
    x = x_ref[rows, :]
    u = _rms(x, n1_ref[...]).astype(BF16)
    yield
    f = _ffn(u, w1i_ref, w1o_ref)
    yield
    h = x + 0.5 * f
    h1_ref[rows, :] = h
    u = _rms(h, nm_ref[...]).astype(BF16)
    yield

    main = _dot(u, win_ref[:, :N_MAIN])

    def proj(lo, hi):
        return main[:, lo:hi]

    o = 0
    qa_dst[rows, :] = proj(o, o + QK) * (DK ** -0.5); o += QK
    ka_dst[rows, :] = proj(o, o + QK); o += QK
    va_dst[rows, :] = proj(o, o + VW).astype(BF16); o += VW
    ra = proj(o, o + VW); o += VW
    ra_dst[rows, :] = (ra * _sigmoid(ra)).astype(BF16)
    cos = tab_ref[rows, :QK]
    sin = tab_ref[rows, QK:]
    qr = proj(o, o + QK); o += QK
    qr_dst[rows, :] = qr * cos + _swap_halves(qr) * sin
    kr = proj(o, o + QK); o += QK
    kr_dst[rows, :] = (kr * cos + _swap_halves(kr) * sin) * (DK ** -0.5)
    vr_dst[rows, :] = proj(o, o + VW).astype(BF16); o += VW
    gr = proj(o, o + VW); o += VW
    gr_dst[rows, :] = (gr * _sigmoid(gr)).astype(BF16)
    yield
    a_low = _dot(u, win_ref[:, N_MAIN:N_MAIN + LOW_RANK])
    z = _dot(a_low.astype(BF16), wup_ref[...]) + balpha_ref[...]
    log_sig = jnp.minimum(z, 0.0) - jnp.log1p(jnp.exp(-jnp.abs(z)))
    la = log_sig * (1.0 / GATE_TAU)
    la_dst[rows, :] = la
    for lo in range(0, la.shape[0], SCAN_CHUNK):
        decay_sums.append(jnp.min(jnp.sum(la[lo:lo + SCAN_CHUNK, :], axis=0, keepdims=True)))
    gg_ref[rows, :] = _sigmoid(_dot(u, wgate_ref[...])).astype(BF16)
    yield


_PRE_ORDER = (0, 0, 1, 0, 1, 0, 1, 0, 1, 1)


def _pre_kernel(x_ref, tab_ref, n1_ref, w1i_ref, w1o_ref, nm_ref, win_ref, wup_ref, balpha_ref,
                h1_ref, gg_ref, qa_ref, ka_ref, va_ref, la_ref, qr_ref, kr_ref, vr_ref, ra_ref, gr_ref,
                wgate_ref):
    @pl.when(pl.program_id(0) == 0)
    def _():
        wgate_ref[...] = win_ref[:, N_MAIN + LOW_RANK:]

    tile = x_ref.shape[0]
    chains = [_pre_chain(slice(lo, lo + tile // 2), x_ref, tab_ref, n1_ref, w1i_ref, w1o_ref, nm_ref,
                         win_ref, wup_ref, balpha_ref, wgate_ref, h1_ref, gg_ref,
                         qa_ref, ka_ref, va_ref, la_ref, qr_ref, kr_ref, vr_ref, ra_ref, gr_ref, [])
              for lo in (0, tile // 2)]
    for c in _PRE_ORDER:
        next(chains[c])


def _pre_call(x, table, weights):
    n, d = x.shape
    tm = PRE_TILE
    table_blocks = table.shape[0] // tm
    row = lambda w: pl.BlockSpec((tm, w), lambda i: (i, 0))
    tab = pl.BlockSpec((tm, 2 * QK), lambda i: (i % table_blocks, 0))
    out_widths = [(d, F32), (2 * d, BF16), (QK, F32), (QK, F32), (VW, BF16), (QK, F32), (QK, F32), (QK, F32),
                  (VW, BF16), (VW, BF16), (VW, BF16)]
    return pl.pallas_call(
        _pre_kernel,
        grid=(n // tm,),
        in_specs=[row(d), tab] + [_whole(w) for w in weights],
        out_specs=[row(w) for w, _ in out_widths],
        out_shape=[jax.ShapeDtypeStruct((n, w), t) for w, t in out_widths],
        scratch_shapes=[pltpu.VMEM((d, 2 * d), BF16)],
        compiler_params=pltpu.CompilerParams(
            dimension_semantics=("arbitrary",), vmem_limit_bytes=VMEM_LIMIT),
        name="pre",
    )(x, table, *weights)


def _front_kernel(x_ref, tab_ref, n1_ref, w1i_ref, w1o_ref, nm_ref, win_ref, wup_ref, balpha_ref, gn_ref,
                  *rest, n_cast, tiles_per_seq):
    cast_in = rest[:n_cast]
    h1_ref, gg_ref, o_ref, sg_ref, sr_ref = rest[n_cast:n_cast + 5]
    cast_out = rest[n_cast + 5:2 * n_cast + 5]
    wgate_ref = rest[2 * n_cast + 5]
    handoff = rest[2 * n_cast + 6:2 * n_cast + 15]
    stg_ref, str_ref, b_ref, acc_ref, ks_ref, bs_ref, vs_ref, decay_ref = rest[2 * n_cast + 15:]
    i = pl.program_id(0)
    slot = i % 2
    written = [ref.at[slot] for ref in handoff]
    qa_s, ka_s, va_s, la_s, qr_s, kr_s, vr_s, ra_s, gr_s = (ref.at[1 - slot] for ref in handoff)
    c_len = SCAN_CHUNK
    tile = x_ref.shape[0]
    n_chunks = tile // c_len
    r = lax.broadcasted_iota(jnp.int32, (c_len, c_len), 0)
    c = lax.broadcasted_iota(jnp.int32, (c_len, c_len), 1)
    causal = c <= r

    def chunk_cumsum(la):
        tril = jnp.where(causal, 1.0, 0.0).astype(BF16)
        hi, mid, lo = _split3(la)
        return _dot(tril, hi) + _dot(tril, mid) + _dot(tril, lo)

    @pl.when(i == 0)
    def _():
        wgate_ref[...] = win_ref[:, N_MAIN + LOW_RANK:]
        for ref in (*handoff, acc_ref):
            ref[...] = jnp.zeros_like(ref)
        decay_ref[0] = 0.0

    @pl.when((i == 0) | (i % tiles_per_seq == 1))
    def _():
        stg_ref[...] = jnp.zeros_like(stg_ref)
        str_ref[...] = jnp.zeros_like(str_ref)

    @pl.when(decay_ref[0] < GLA_SAFE_LOG_DECAY)
    def _():
        seg = _head_segment_sum()
        in_chunk = lax.broadcasted_iota(jnp.int32, (tile, QK), 0) % c_len
        for n in range(n_chunks):
            rows = slice(n * c_len, (n + 1) * c_len)
            b_ref[rows, :] = chunk_cumsum(la_s[rows, :])
        ks_ref[...] = ka_s[...]
        bs_ref[...] = b_ref[...]
        vs_ref[...] = va_s[...].astype(F32)
        acc_ref[...] = jnp.zeros_like(acc_ref)

        def body(d, carry):
            ok = in_chunk >= d
            k_sh, b_sh, v_sh = ks_ref[...], bs_ref[...], vs_ref[...]
            w = jnp.exp(jnp.where(ok, b_ref[...] - b_sh, 0.0))
            p = jnp.where(ok, qa_s[...] * k_sh * w, 0.0).astype(BF16)
            acc_ref[...] += _dot(p, seg) * v_sh
            ks_ref[...] = pltpu.roll(k_sh, 1, 0)
            bs_ref[...] = pltpu.roll(b_sh, 1, 0)
            vs_ref[...] = pltpu.roll(v_sh, 1, 0)
            return carry

        lax.fori_loop(0, c_len, body, 0)

    for src, dst in zip(cast_in, cast_out):
        dst[...] = src[...].astype(BF16)

    def scan():
        exact = decay_ref[0] < GLA_SAFE_LOG_DECAY
        rr = lax.broadcasted_iota(jnp.int32, (tile, tile), 0)
        cc = lax.broadcasted_iota(jnp.int32, (tile, tile), 1)
        ret_causal = cc <= rr
        dist = (rr - cc).astype(F32)
        lg_lane = _head_lane_const(RET_LOG_DECAY, QK, DK)
        row_qk = lax.broadcasted_iota(jnp.int32, (tile, QK), 0).astype(F32)
        k_decay = jnp.exp(lg_lane * (tile - 1.0 - row_qk))
        q_decay = jnp.exp(lg_lane * (row_qk + 1.0))
        tile_decay = jnp.exp(lg_lane * float(tile))
        gla = []
        for n in range(n_chunks):
            rows = slice(n * c_len, (n + 1) * c_len)
            b = chunk_cumsum(la_s[rows, :])
            b_last = b[c_len - 1:c_len, :]
            k = ka_s[rows, :]
            qt = (qa_s[rows, :] * jnp.exp(b)).astype(BF16)
            kt = (k * jnp.exp(jnp.minimum(-b, -GLA_SAFE_LOG_DECAY))).astype(BF16)
            kd = (k * jnp.exp(b_last - b)).astype(BF16)
            gla.append((qt, kt, kd, jnp.exp(b_last), va_s[rows, :]))
        kf = kr_s[...]
        qf = qr_s[...]
        rq, rqd = qf.astype(BF16), (qf * q_decay).astype(BF16)
        rk, rkd = kf.astype(BF16), (kf * k_decay).astype(BF16)
        rv = vr_s[...]
        yield
        for n in range(n_chunks):
            rows = slice(n * c_len, (n + 1) * c_len)
            qt, kt, kd, e_last, v_all = gla[n]
            st = stg_ref[...]
            st_b = st.astype(BF16)
            stg_ref[...] = st * e_last
            for h in range(N_HEADS):
                ks = slice(h * DK, (h + 1) * DK)
                vs = slice(h * DV, (h + 1) * DV)
                v = v_all[:, vs]
                s = jnp.where(causal, _dot_nt(qt[:, ks], kt[:, ks]), 0.0).astype(BF16)
                intra = jnp.where(exact, acc_ref[rows, vs], _dot(s, v))
                o_ref[rows, vs] = _gated_norm(intra + _dot_nt(qt[:, ks], st_b[:, ks]), gn_ref[:, vs], ra_s[rows, vs])
                stg_ref[:, ks] += _dot_tn(v, kd[:, ks])
            yield
        st = str_ref[...]
        st_b = st.astype(BF16)
        str_ref[...] = st * tile_decay
        for h in range(N_HEADS):
            ks = slice(h * DK, (h + 1) * DK)
            vs = slice(h * DV, (h + 1) * DV)
            v = rv[:, vs]
            lg = RET_LOG_DECAY[h]
            s = (_dot_nt(rq[:, ks], rk[:, ks]) * jnp.where(ret_causal, jnp.exp(lg * dist), 0.0)).astype(BF16)
            cols = slice(VW + h * DV, VW + (h + 1) * DV)
            o_ref[:, cols] = _gated_norm(_dot(s, v) + _dot_nt(rqd[:, ks], st_b[:, ks]),
                                         gn_ref[:, cols], gr_s[:, vs])
            str_ref[:, ks] += _dot_tn(v, rkd[:, ks])
        yield

    last = pl.num_programs(0) - 1

    @pl.when(i < last)
    def _():
        decay_sums = []
        chains = [_pre_chain(slice(lo, lo + tile // FRONT_PRE_CHAINS), x_ref, tab_ref, n1_ref, w1i_ref, w1o_ref,
                             nm_ref, win_ref, wup_ref, balpha_ref, wgate_ref, h1_ref, gg_ref, *written, decay_sums)
                  for lo in range(0, tile, tile // FRONT_PRE_CHAINS)]
        chains.append(scan())
        for ch in _FRONT_ORDER:
            next(chains[ch])
        decay_ref[0] = functools.reduce(jnp.minimum, decay_sums)

    @pl.when(i == last)
    def _():
        for _ in scan():
            pass

    @pl.when((i > 0) & (i % tiles_per_seq == 0))
    def _():
        for h in range(N_HEADS):
            ks = slice(h * DK, (h + 1) * DK)
            sg_ref[0, h] = stg_ref[:, ks].T
            sr_ref[0, h] = str_ref[:, ks].T


FRONT_PRE_CHAINS = 1
_FRONT_ORDER = (1, 0, 0, 1, 0, 1, 0, 0)


def _cast_blocks(rows, steps):
    return max(n for n in range(1, steps + 1) if rows % n == 0 and (rows // n) % BF16_ROWS == 0)


def _front_call(x, table, weights, to_cast, batch, seq):
    n, d = x.shape
    tm = PRE_TILE
    assert tm % SCAN_CHUNK == 0 and seq % tm == 0
    n_tiles = n // tm
    tiles_per_seq = seq // tm
    table_blocks = table.shape[0] // tm
    cur = lambda i: jnp.minimum(i, n_tiles - 1)
    prev = lambda i: jnp.maximum(i - 1, 0)
    row = lambda w: pl.BlockSpec((tm, w), lambda i: (cur(i), 0))
    tab = pl.BlockSpec((tm, 2 * QK), lambda i: (cur(i) % table_blocks, 0))
    state = pl.BlockSpec((1, N_HEADS, DK, DV), lambda i: (prev(i) // tiles_per_seq, 0, 0, 0))
    state_shape = jax.ShapeDtypeStruct((batch, N_HEADS, DK, DV), F32)
    out_widths = [(d, F32), (2 * d, BF16)]

    def slab(w):
        blocks = _cast_blocks(w.shape[0], n_tiles)
        return pl.BlockSpec((w.shape[0] // blocks, w.shape[1]), lambda i: (jnp.minimum(i, blocks - 1), 0))

    cast_specs = [slab(w) for w in to_cast]
    vm = lambda rows, w, t: pltpu.VMEM((rows, w), t)
    two = lambda w, t: pltpu.VMEM((2, tm, w), t)
    outs = pl.pallas_call(
        functools.partial(_front_kernel, n_cast=len(to_cast), tiles_per_seq=tiles_per_seq),
        grid=(n_tiles + 1,),
        in_specs=[row(d), tab] + [_whole(w) for w in weights] + cast_specs,
        out_specs=[row(w) for w, _ in out_widths]
        + [pl.BlockSpec((tm, 2 * VW), lambda i: (prev(i), 0)), state, state] + cast_specs,
        out_shape=[jax.ShapeDtypeStruct((n, w), t) for w, t in out_widths]
        + [jax.ShapeDtypeStruct((n, 2 * VW), BF16), state_shape, state_shape]
        + [jax.ShapeDtypeStruct(w.shape, BF16) for w in to_cast],
        scratch_shapes=[vm(d, 2 * d, BF16),
                        two(QK, F32), two(QK, F32), two(VW, BF16), two(QK, F32),
                        two(QK, F32), two(QK, F32), two(VW, BF16), two(VW, BF16), two(VW, BF16),
                        vm(DV, QK, F32), vm(DV, QK, F32),
                        vm(tm, QK, F32), vm(tm, VW, F32), vm(tm, QK, F32), vm(tm, QK, F32), vm(tm, VW, F32),
                        pltpu.SMEM((1,), F32)],
        compiler_params=pltpu.CompilerParams(
            dimension_semantics=("arbitrary",), vmem_limit_bytes=VMEM_LIMIT),
        name="front",
    )(x, table, *weights, *to_cast)
    return outs[:5], outs[5:]


def _sample_scan_kernel(qa_ref, ka_ref, la_ref, va_ref, qr_ref, kr_ref, vr_ref, ra_ref, gr_ref, gn_ref,
                        sg0_ref, sr0_ref, o_ref, sg_ref, sr_ref, raw_ref, *, steps):
    rows = qa_ref.shape[0]
    n_seq = rows // steps
    pair = 8 // steps
    t_qk = lax.broadcasted_iota(jnp.int32, (rows, QK), 0) % steps
    t_v = lax.broadcasted_iota(jnp.int32, (rows, VW), 0) % steps
    seg = _head_segment_sum()

    def down(x, d, t):
        return x if d == 0 else jnp.where(t >= d, pltpu.roll(x, d, 0), 0.0)

    g = la_ref[...]
    b = g
    for d in range(1, steps):
        b = b + down(g, d, t_qk)
    last = jnp.where(t_qk == steps - 1, b, 0.0)
    b_last = last
    for d in range(1, steps):
        b_last = b_last + pltpu.roll(last, rows - d, 0)

    qa = qa_ref[...]
    ka = ka_ref[...]
    va = va_ref[...].astype(F32)
    qr = qr_ref[...]
    kr = kr_ref[...]
    vr = vr_ref[...].astype(F32)
    lg_lane = _head_lane_const(RET_LOG_DECAY, QK, DK)
    lg_v = _head_lane_const(RET_LOG_DECAY, VW, DV)

    oa = jnp.zeros((rows, VW), F32)
    orr = jnp.zeros((rows, VW), F32)
    for d in range(steps):
        ok = t_qk >= d
        w = jnp.where(ok, jnp.exp(jnp.where(ok, b - down(b, d, t_qk), 0.0)), 0.0)
        pa = (qa * down(ka, d, t_qk) * w).astype(BF16)
        oa = oa + _dot(pa, seg) * down(va, d, t_v)
        pr = (qr * down(kr, d, t_qk) * jnp.where(ok, jnp.exp(lg_lane * float(d)), 0.0)).astype(BF16)
        orr = orr + _dot(pr, seg) * down(vr, d, t_v)

    qta = (qa * jnp.exp(b)).astype(BF16)
    kda = (ka * jnp.exp(b_last - b)).astype(BF16)
    e_hi, e_mid, e_lo = _split3(jnp.exp(b_last))
    zero = jnp.zeros_like(e_hi)
    e_parts = jnp.where(t_qk == 0, e_hi, jnp.where(t_qk == 1, e_mid, jnp.where(t_qk == 2, e_lo, zero)))
    qtr = qr.astype(BF16)
    q_scale = jnp.exp(lg_v * (t_v.astype(F32) + 1.0))
    kdr = (kr * jnp.exp(lg_lane * (steps - 1.0 - t_qk.astype(F32)))).astype(BF16)
    ones = jnp.ones((8, DV), BF16)
    slab_row = lax.broadcasted_iota(jnp.int32, (8, 1), 0) // steps
    va_b = va_ref[...]
    vr_b = vr_ref[...]

    for p in range(n_seq // pair):
        slab = slice(8 * p, 8 * p + 8)
        for h in range(N_HEADS):
            ks = slice(h * DK, (h + 1) * DK)
            vs = slice(h * DV, (h + 1) * DV)
            inter_a = jnp.zeros((8, DV), F32)
            inter_r = jnp.zeros((8, DV), F32)
            for j in range(pair):
                s_idx = p * pair + j
                mine = slab_row == j
                s0a = sg0_ref[s_idx, h]
                s0r = sr0_ref[s_idx, h]
                inter_a = jnp.where(mine, _dot(qta[slab, ks], s0a.astype(BF16)), inter_a)
                inter_r = jnp.where(mine, _dot(qtr[slab, ks], s0r.astype(BF16)), inter_r)
                e_col = _dot_tn(jnp.where(mine, e_parts[slab, ks], 0.0).astype(BF16), ones)
                kd = jnp.where(mine, kda[slab, ks], 0.0).astype(BF16)
                sg_ref[s_idx, h] = e_col * s0a + _dot_tn(kd, va_b[slab, vs])
                kd = jnp.where(mine, kdr[slab, ks], 0.0).astype(BF16)
                sr_ref[s_idx, h] = math.exp(RET_LOG_DECAY[h] * steps) * s0r + _dot_tn(kd, vr_b[slab, vs])
            raw_ref[slab, vs] = oa[slab, vs] + inter_a
            raw_ref[slab, VW + h * DV:VW + (h + 1) * DV] = orr[slab, vs] + q_scale[slab, vs] * inter_r

    for h in range(N_HEADS):
        vs = slice(h * DV, (h + 1) * DV)
        cols = slice(VW + h * DV, VW + (h + 1) * DV)
        o_ref[:, vs] = _gated_norm(raw_ref[:, vs], gn_ref[:, vs], ra_ref[:, vs])
        o_ref[:, cols] = _gated_norm(raw_ref[:, cols], gn_ref[:, cols], gr_ref[:, vs])


def _sample_scan_call(qa, ka, la, va, qr, kr, vr, ra, gr, gn, sg0, sr0, steps):
    n_seq = sg0.shape[0]
    rows = SAMPLE_SEQS * steps
    row = lambda w: pl.BlockSpec((rows, w), lambda i: (i, 0))
    state = pl.BlockSpec((SAMPLE_SEQS, N_HEADS, DK, DV), lambda i: (i, 0, 0, 0))
    state_shape = jax.ShapeDtypeStruct(sg0.shape, F32)
    return pl.pallas_call(
        functools.partial(_sample_scan_kernel, steps=steps),
        grid=(n_seq // SAMPLE_SEQS,),
        in_specs=[row(QK), row(QK), row(QK), row(VW), row(QK), row(QK), row(VW), row(VW), row(VW), _whole(gn),
                  state, state],
        out_specs=[row(2 * VW), state, state],
        out_shape=[jax.ShapeDtypeStruct((n_seq * steps, 2 * VW), BF16), state_shape, state_shape],
        scratch_shapes=[pltpu.VMEM((rows, 2 * VW), F32)],
        compiler_params=pltpu.CompilerParams(
            dimension_semantics=("arbitrary",), vmem_limit_bytes=VMEM_LIMIT),
        name="sample_scan",
    )(qa, ka, la, va, qr, kr, vr, ra, gr, gn, sg0, sr0)


def _post_chain(rows, o_ref, h1_ref, gg_ref, p_ref,
                wout_ref, n2_ref, w2i_ref, w2o_ref, npl_ref, wpg_ref, wpp_ref, nf_ref, y_ref, final):
    d = h1_ref.shape[1]
    mix = (gg_ref[rows, :d].astype(F32) * _dot(o_ref[rows, :VW], wout_ref[:VW, :])
           + gg_ref[rows, d:].astype(F32) * _dot(o_ref[rows, VW:], wout_ref[VW:, :]))
    yield
    h = h1_ref[rows, :] + mix
    u = _rms(h, n2_ref[...]).astype(BF16)
    yield
    f = _ffn(u, w2i_ref, w2o_ref)
    yield
    h = h + 0.5 * f
    u = _rms(h, npl_ref[...]).astype(BF16)
    yield
    gate = _dot(u, wpg_ref[...])
    proj = _dot(p_ref[rows, :].astype(BF16), wpp_ref[...])
    yield
    h = h + proj * _sigmoid(gate)
    y_ref[rows, :] = _rms(h, nf_ref[...]) if final else h
    yield


_POST_ORDER = (0, 0, 1, 0, 1, 1, 0, 0, 1, 1, 0, 1)


def _post_kernel(*refs, final):
    tile = refs[0].shape[0]
    chains = [_post_chain(slice(lo, lo + tile // 2), *refs, final) for lo in (0, tile // 2)]
    for c in _POST_ORDER:
        next(chains[c])


def _post_call(o, h1, gg, p, weights, final):
    n, d = h1.shape
    tm = POST_TILE
    row = lambda w: pl.BlockSpec((tm, w), lambda i: (i, 0))
    return pl.pallas_call(
        functools.partial(_post_kernel, final=final),
        grid=(n // tm,),
        in_specs=[row(2 * VW), row(d), row(2 * d), row(p.shape[1])] + [_whole(w) for w in weights],
        out_specs=row(d),
        out_shape=jax.ShapeDtypeStruct((n, d), F32),
        compiler_params=pltpu.CompilerParams(
            dimension_semantics=("arbitrary",), vmem_limit_bytes=VMEM_LIMIT),
        name="post",
    )(o, h1, gg, p, *weights)


def _rotary_table(pos, rows):
    half = DK // 2
    freq = ROPE_BASE ** (-np.arange(half, dtype=np.float64) / half)
    ang = np.asarray(pos, np.float64)[:, None] * freq[None, :]
    cos, sin = np.cos(ang), np.sin(ang)
    reps = (rows // len(pos), N_HEADS)
    return jnp.asarray(np.concatenate([np.tile(np.concatenate([cos, cos], axis=-1), reps),
                                       np.tile(np.concatenate([-sin, sin], axis=-1), reps)], axis=-1), F32)


def kernel(x_prompt, x_sample, state_gla, state_ret, p_prompt, p_sample, norm_ffn1, w_ffn1_in, w_ffn1_out, norm_mix, w_in, w_alpha_up, b_alpha, gn_gla, gn_ret, w_out, norm_ffn2, w_ffn2_in, w_ffn2_out, norm_ple, w_ple_gate, w_ple_proj, norm_final):
    bp, tp, d = x_prompt.shape
    bs, ts, _ = x_sample.shape
    depth = w_in.shape[0]
    assert w_in.shape[2] == N_MAIN + LOW_RANK + 2 * d
    assert tp % PRE_TILE == 0 and tp % POST_TILE == 0 and tp % REF_CHUNK == 0 and 3 <= ts < REF_CHUNK and 8 % ts == 0
    assert (bs * ts) % POST_TILE == 0 and (bs * ts) % PRE_TILE == 0 and bs % SAMPLE_SEQS == 0

    table_p = _rotary_table(np.arange(tp), tp)
    table_s = _rotary_table(PAST_LEN + np.arange(ts), PRE_TILE)

    hp = x_prompt.reshape(bp * tp, d)
    hs = x_sample.reshape(bs * ts, d)
    row = lambda v: v.reshape(1, -1).astype(F32)
    gla_p, ret_p, gla_s, ret_s = [], [], [], []
    for i in range(depth):
        pre_w = (row(norm_ffn1[i]), w_ffn1_in[i].astype(BF16), w_ffn1_out[i].astype(BF16), row(norm_mix[i]),
                 w_in[i].astype(BF16), w_alpha_up[i].astype(BF16), row(b_alpha[i]))
        post_cast = (w_out[i], w_ffn2_in[i], w_ffn2_out[i], w_ple_gate[i], w_ple_proj[i])
        final = i == depth - 1

        gn = row(jnp.concatenate([gn_gla[i], gn_ret[i]]))
        (h1, gg, o, sg, sr), post_bf = _front_call(hp, table_p, (*pre_w, gn), post_cast, bp, tp)
        wout_b, w2i_b, w2o_b, wpg_b, wpp_b = post_bf
        post_w = (wout_b, row(norm_ffn2[i]), w2i_b, w2o_b, row(norm_ple[i]), wpg_b, wpp_b, row(norm_final))
        hp = _post_call(o, h1, gg, p_prompt[i].reshape(bp * tp, -1), post_w, final)
        gla_p.append(sg.astype(state_gla.dtype))
        ret_p.append(sr.astype(state_ret.dtype))

        h1, gg, qa, ka, va, la, qr, kr, vr, ra, gr = _pre_call(hs, table_s, pre_w)
        o, sg, sr = _sample_scan_call(qa, ka, la, va, qr, kr, vr, ra, gr, gn,
                                      state_gla[i].astype(F32), state_ret[i].astype(F32), ts)
        hs = _post_call(o, h1, gg, p_sample[i].reshape(bs * ts, -1), post_w, final)
        gla_s.append(sg.astype(state_gla.dtype))
        ret_s.append(sr.astype(state_ret.dtype))

    return (hp.reshape(bp, tp, d), hs.reshape(bs, ts, d),
            jnp.stack(gla_p), jnp.stack(ret_p), jnp.stack(gla_s), jnp.stack(ret_s))
```

```python
import functools
import math

import numpy as np

import jax
import jax.numpy as jnp
from jax import lax
from jax.experimental import pallas as pl
from jax.experimental.pallas import tpu as pltpu

F32 = jnp.float32
BF16 = jnp.bfloat16

N_HEADS = 4
DK = 64
DV = 128
QK = N_HEADS * DK
VW = N_HEADS * DV
N_MAIN = 4 * QK + 4 * VW
LOW_RANK = 16
GATE_TAU = 16.0
ROPE_BASE = 10000.0
PAST_LEN = 16384
REF_CHUNK = 64
EPS = 1e-6
RET_LOG_DECAY = tuple(math.log1p(-(2.0 ** (-5.0 - h))) for h in range(N_HEADS))

PRE_TILE = 256
POST_TILE = 512
SCAN_CHUNK = 256
SAMPLE_SEQS = 16
BF16_ROWS = 16
MXU_TILE = 256
VMEM_LIMIT = 56 * 1024 * 1024
GLA_SAFE_LOG_DECAY = -60.0

NT = (((1,), (1,)), ((), ()))
TN = (((0,), (0,)), ((), ()))


def _dot(a, b):
    return jnp.dot(a, b, preferred_element_type=F32)


def _dot_nt(a, b):
    return lax.dot_general(a, b, NT, preferred_element_type=F32)


def _dot_tn(a, b):
    return lax.dot_general(a, b, TN, preferred_element_type=F32)


def _rms(x, w):
    return x * lax.rsqrt(jnp.mean(x * x, axis=-1, keepdims=True) + EPS) * w


def _sigmoid(x):
    return jax.nn.sigmoid(x)


def _ffn(u, wi_ref, wo_ref):
    d_ff = wo_ref.shape[0]
    cut = -(-d_ff // (2 * MXU_TILE)) * MXU_TILE
    out = None
    for lo, hi in ((0, cut), (cut, d_ff)):
        a = _dot(u, wi_ref[:, lo:hi])
        b = _dot(u, wi_ref[:, d_ff + lo:d_ff + hi])
        g = (a * _sigmoid(a) * b).astype(BF16)
        part = _dot(g, wo_ref[lo:hi, :])
        out = part if out is None else out + part
    return out


def _split3(x):
    hi = x.astype(BF16)
    r1 = x - hi.astype(F32)
    mid = r1.astype(BF16)
    lo = (r1 - mid.astype(F32)).astype(BF16)
    return hi, mid, lo


def _head_lane_const(values, width, per_head):
    lane = lax.broadcasted_iota(jnp.int32, (1, width), 1) // per_head
    out = jnp.full((1, width), values[-1], F32)
    for h in range(len(values) - 2, -1, -1):
        out = jnp.where(lane == h, values[h], out)
    return out


def _head_segment_sum():
    lane_head = lax.broadcasted_iota(jnp.int32, (QK, VW), 0) // DK
    col_head = lax.broadcasted_iota(jnp.int32, (QK, VW), 1) // DV
    return jnp.where(lane_head == col_head, 1.0, 0.0).astype(BF16)


def _swap_halves(x):
    n = x.shape[-1]
    lane = lax.broadcasted_iota(jnp.int32, x.shape, 1)
    fwd = pltpu.roll(x, n - DK // 2, 1)
    bwd = pltpu.roll(x, DK // 2, 1)
    return jnp.where(lane % DK < DK // 2, fwd, bwd)


def _gated_norm(o, w, gate):
    return (o * lax.rsqrt(jnp.mean(o * o, axis=-1, keepdims=True) + EPS) * w * gate.astype(F32)).astype(BF16)


def _whole(_):
    return pl.BlockSpec(memory_space=pltpu.VMEM)


def _pre_chain(rows, x_ref, tab_ref, n1_ref, w1i_ref, w1o_ref, nm_ref, win_ref, wup_ref, balpha_ref,
               wgate_ref, h1_ref, gg_ref,
               qa_dst, ka_dst, va_dst, la_dst, qr_dst, kr_dst, vr_dst, ra_dst, gr_dst, decay_sums):
    x = x_ref[rows, :]
    u = _rms(x, n1_ref[...]).astype(BF16)
    yield
    f = _ffn(u, w1i_ref, w1o_ref)
    yield
    h = x + 0.5 * f
    h1_ref[rows, :] = h
    u = _rms(h, nm_ref[...]).astype(BF16)
    yield

    main = _dot(u, win_ref[:, :N_MAIN])

    def proj(lo, hi):
        return main[:, lo:hi]

    o = 0
    qa_dst[rows, :] = proj(o, o + QK) * (DK ** -0.5); o += QK
    ka_dst[rows, :] = proj(o, o + QK); o += QK
    va_dst[rows, :] = proj(o, o + VW).astype(BF16); o += VW
    ra = proj(o, o + VW); o += VW
    ra_dst[rows, :] = (ra * _sigmoid(ra)).astype(BF16)
    cos = tab_ref[rows, :QK]
    sin = tab_ref[rows, QK:]
    qr = proj(o, o + QK); o += QK
    qr_dst[rows, :] = qr * cos + _swap_halves(qr) * sin
    kr = proj(o, o + QK); o += QK
    kr_dst[rows, :] = (kr * cos + _swap_halves(kr) * sin) * (DK ** -0.5)
    vr_dst[rows, :] = proj(o, o + VW).astype(BF16); o += VW
    gr = proj(o, o + VW); o += VW
    gr_dst[rows, :] = (gr * _sigmoid(gr)).astype(BF16)
    yield
    a_low = _dot(u, win_ref[:, N_MAIN:N_MAIN + LOW_RANK])
    z = _dot(a_low.astype(BF16), wup_ref[...]) + balpha_ref[...]
    log_sig = jnp.minimum(z, 0.0) - jnp.log1p(jnp.exp(-jnp.abs(z)))
    la = log_sig * (1.0 / GATE_TAU)
    la_dst[rows, :] = la
    for lo in range(0, la.shape[0], SCAN_CHUNK):
        decay_sums.append(jnp.min(jnp.sum(la[lo:lo + SCAN_CHUNK, :], axis=0, keepdims=True)))
    gg_ref[rows, :] = _sigmoid(_dot(u, wgate_ref[...])).astype(BF16)
    yield


_PRE_ORDER = (0, 0, 1, 0, 1, 0, 1, 0, 1, 1)


def _pre_kernel(x_ref, tab_ref, n1_ref, w1i_ref, w1o_ref, nm_ref, win_ref, wup_ref, balpha_ref,
                h1_ref, gg_ref, qa_ref, ka_ref, va_ref, la_ref, qr_ref, kr_ref, vr_ref, ra_ref, gr_ref,
                wgate_ref):
    @pl.when(pl.program_id(0) == 0)
    def _():
        wgate_ref[...] = win_ref[:, N_MAIN + LOW_RANK:]

    tile = x_ref.shape[0]
    chains = [_pre_chain(slice(lo, lo + tile // 2), x_ref, tab_ref, n1_ref, w1i_ref, w1o_ref, nm_ref,
                         win_ref, wup_ref, balpha_ref, wgate_ref, h1_ref, gg_ref,
                         qa_ref, ka_ref, va_ref, la_ref, qr_ref, kr_ref, vr_ref, ra_ref, gr_ref, [])
              for lo in (0, tile // 2)]
    for c in _PRE_ORDER:
        next(chains[c])


def _pre_call(x, table, weights):
    n, d = x.shape
    tm = PRE_TILE
    table_blocks = table.shape[0] // tm
    row = lambda w: pl.BlockSpec((tm, w), lambda i: (i, 0))
    tab = pl.BlockSpec((tm, 2 * QK), lambda i: (i % table_blocks, 0))
    out_widths = [(d, F32), (2 * d, BF16), (QK, F32), (QK, F32), (VW, BF16), (QK, F32), (QK, F32), (QK, F32),
                  (VW, BF16), (VW, BF16), (VW, BF16)]
    return pl.pallas_call(
        _pre_kernel,
        grid=(n // tm,),
        in_specs=[row(d), tab] + [_whole(w) for w in weights],
        out_specs=[row(w) for w, _ in out_widths],
        out_shape=[jax.ShapeDtypeStruct((n, w), t) for w, t in out_widths],
        scratch_shapes=[pltpu.VMEM((d, 2 * d), BF16)],
        compiler_params=pltpu.CompilerParams(
            dimension_semantics=("arbitrary",), vmem_limit_bytes=VMEM_LIMIT),
        name="pre",
    )(x, table, *weights)


def _front_kernel(x_ref, tab_ref, n1_ref, w1i_ref, w1o_ref, nm_ref, win_ref, wup_ref, balpha_ref, gn_ref,
                  *rest, n_cast, tiles_per_seq):
    cast_in = rest[:n_cast]
    h1_ref, gg_ref, o_ref, sg_ref, sr_ref = rest[n_cast:n_cast + 5]
    cast_out = rest[n_cast + 5:2 * n_cast + 5]
    wgate_ref = rest[2 * n_cast + 5]
    handoff = rest[2 * n_cast + 6:2 * n_cast + 15]
    stg_ref, str_ref, b_ref, acc_ref, ks_ref, bs_ref, vs_ref, decay_ref = rest[2 * n_cast + 15:]
    i = pl.program_id(0)
    slot = i % 2
    written = [ref.at[slot] for ref in handoff]
    qa_s, ka_s, va_s, la_s, qr_s, kr_s, vr_s, ra_s, gr_s = (ref.at[1 - slot] for ref in handoff)
    c_len = SCAN_CHUNK
    tile = x_ref.shape[0]
    n_chunks = tile // c_len
    r = lax.broadcasted_iota(jnp.int32, (c_len, c_len), 0)
    c = lax.broadcasted_iota(jnp.int32, (c_len, c_len), 1)
    causal = c <= r

    def chunk_cumsum(la):
        tril = jnp.where(causal, 1.0, 0.0).astype(BF16)
        hi, mid, _ = _split3(la)
        return _dot(tril, hi) + _dot(tril, mid)

    @pl.when(i == 0)
    def _():
        wgate_ref[...] = win_ref[:, N_MAIN + LOW_RANK:]
        for ref in (*handoff, acc_ref):
            ref[...] = jnp.zeros_like(ref)
        decay_ref[0] = 0.0

    @pl.when((i == 0) | (i % tiles_per_seq == 1))
    def _():
        stg_ref[...] = jnp.zeros_like(stg_ref)
        str_ref[...] = jnp.zeros_like(str_ref)

    @pl.when(decay_ref[0] < GLA_SAFE_LOG_DECAY)
    def _():
        seg = _head_segment_sum()
        in_chunk = lax.broadcasted_iota(jnp.int32, (tile, QK), 0) % c_len
        for n in range(n_chunks):
            rows = slice(n * c_len, (n + 1) * c_len)
            b_ref[rows, :] = chunk_cumsum(la_s[rows, :])
        ks_ref[...] = ka_s[...]
        bs_ref[...] = b_ref[...]
        vs_ref[...] = va_s[...].astype(F32)
        acc_ref[...] = jnp.zeros_like(acc_ref)

        def body(d, carry):
            ok = in_chunk >= d
            k_sh, b_sh, v_sh = ks_ref[...], bs_ref[...], vs_ref[...]
            w = jnp.exp(jnp.where(ok, b_ref[...] - b_sh, 0.0))
            p = jnp.where(ok, qa_s[...] * k_sh * w, 0.0).astype(BF16)
            acc_ref[...] += _dot(p, seg) * v_sh
            ks_ref[...] = pltpu.roll(k_sh, 1, 0)
            bs_ref[...] = pltpu.roll(b_sh, 1, 0)
            vs_ref[...] = pltpu.roll(v_sh, 1, 0)
            return carry

        lax.fori_loop(0, c_len, body, 0)

    for src, dst in zip(cast_in, cast_out):
        dst[...] = src[...].astype(BF16)

    def scan():
        exact = decay_ref[0] < GLA_SAFE_LOG_DECAY
        rr = lax.broadcasted_iota(jnp.int32, (tile, tile), 0)
        cc = lax.broadcasted_iota(jnp.int32, (tile, tile), 1)
        ret_causal = cc <= rr
        dist = (rr - cc).astype(F32)
        lg_lane = _head_lane_const(RET_LOG_DECAY, QK, DK)
        row_qk = lax.broadcasted_iota(jnp.int32, (tile, QK), 0).astype(F32)
        k_decay = jnp.exp(lg_lane * (tile - 1.0 - row_qk))
        q_decay = jnp.exp(lg_lane * (row_qk + 1.0))
        tile_decay = jnp.exp(lg_lane * float(tile))
        gla = []
        for n in range(n_chunks):
            rows = slice(n * c_len, (n + 1) * c_len)
            b = chunk_cumsum(la_s[rows, :])
            b_last = b[c_len - 1:c_len, :]
            k = ka_s[rows, :]
            qt = (qa_s[rows, :] * jnp.exp(b)).astype(BF16)
            kt = (k * jnp.exp(jnp.minimum(-b, -GLA_SAFE_LOG_DECAY))).astype(BF16)
            kd = (k * jnp.exp(b_last - b)).astype(BF16)
            gla.append((qt, kt, kd, jnp.exp(b_last), va_s[rows, :]))
        kf = kr_s[...]
        qf = qr_s[...]
        rq, rqd = qf.astype(BF16), (qf * q_decay).astype(BF16)
        rk, rkd = kf.astype(BF16), (kf * k_decay).astype(BF16)
        rv = vr_s[...]
        yield
        for n in range(n_chunks):
            rows = slice(n * c_len, (n + 1) * c_len)
            qt, kt, kd, e_last, v_all = gla[n]
            st = stg_ref[...]
            st_b = st.astype(BF16)
            stg_ref[...] = st * e_last
            for h in range(N_HEADS):
                ks = slice(h * DK, (h + 1) * DK)
                vs = slice(h * DV, (h + 1) * DV)
                v = v_all[:, vs]
                s = jnp.where(causal, _dot_nt(qt[:, ks], kt[:, ks]), 0.0).astype(BF16)
                intra = jnp.where(exact, acc_ref[rows, vs], _dot(s, v))
                o_ref[rows, vs] = _gated_norm(intra + _dot_nt(qt[:, ks], st_b[:, ks]), gn_ref[:, vs], ra_s[rows, vs])
                stg_ref[:, ks] += _dot_tn(v, kd[:, ks])
            yield
        st = str_ref[...]
        st_b = st.astype(BF16)
        str_ref[...] = st * tile_decay
        for h in range(N_HEADS):
            ks = slice(h * DK, (h + 1) * DK)
            vs = slice(h * DV, (h + 1) * DV)
            v = rv[:, vs]
            lg = RET_LOG_DECAY[h]
            s = (_dot_nt(rq[:, ks], rk[:, ks]) * jnp.where(ret_causal, jnp.exp(lg * dist), 0.0)).astype(BF16)
            cols = slice(VW + h * DV, VW + (h + 1) * DV)
            o_ref[:, cols] = _gated_norm(_dot(s, v) + _dot_nt(rqd[:, ks], st_b[:, ks]),
                                         gn_ref[:, cols], gr_s[:, vs])
            str_ref[:, ks] += _dot_tn(v, rkd[:, ks])
        yield

    last = pl.num_programs(0) - 1

    @pl.when(i < last)
    def _():
        decay_sums = []
        chains = [_pre_chain(slice(lo, lo + tile // FRONT_PRE_CHAINS), x_ref, tab_ref, n1_ref, w1i_ref, w1o_ref,
                             nm_ref, win_ref, wup_ref, balpha_ref, wgate_ref, h1_ref, gg_ref, *written, decay_sums)
                  for lo in range(0, tile, tile // FRONT_PRE_CHAINS)]
        chains.append(scan())
        for ch in _FRONT_ORDER:
            next(chains[ch])
        decay_ref[0] = functools.reduce(jnp.minimum, decay_sums)

    @pl.when(i == last)
    def _():
        for _ in scan():
            pass

    @pl.when((i > 0) & (i % tiles_per_seq == 0))
    def _():
        for h in range(N_HEADS):
            ks = slice(h * DK, (h + 1) * DK)
            sg_ref[0, h] = stg_ref[:, ks].T
            sr_ref[0, h] = str_ref[:, ks].T


FRONT_PRE_CHAINS = 1
_FRONT_ORDER = (1, 0, 0, 1, 0, 1, 0, 0)


def _cast_blocks(rows, steps):
    return max(n for n in range(1, steps + 1) if rows % n == 0 and (rows // n) % BF16_ROWS == 0)


def _front_call(x, table, weights, to_cast, batch, seq):
    n, d = x.shape
    tm = PRE_TILE
    assert tm % SCAN_CHUNK == 0 and seq % tm == 0
    n_tiles = n // tm
    tiles_per_seq = seq // tm
    table_blocks = table.shape[0] // tm
    cur = lambda i: jnp.minimum(i, n_tiles - 1)
    prev = lambda i: jnp.maximum(i - 1, 0)
    row = lambda w: pl.BlockSpec((tm, w), lambda i: (cur(i), 0))
    tab = pl.BlockSpec((tm, 2 * QK), lambda i: (cur(i) % table_blocks, 0))
    state = pl.BlockSpec((1, N_HEADS, DK, DV), lambda i: (prev(i) // tiles_per_seq, 0, 0, 0))
    state_shape = jax.ShapeDtypeStruct((batch, N_HEADS, DK, DV), F32)
    out_widths = [(d, F32), (2 * d, BF16)]

    def slab(w):
        blocks = _cast_blocks(w.shape[0], n_tiles)
        return pl.BlockSpec((w.shape[0] // blocks, w.shape[1]), lambda i: (jnp.minimum(i, blocks - 1), 0))

    cast_specs = [slab(w) for w in to_cast]
    vm = lambda rows, w, t: pltpu.VMEM((rows, w), t)
    two = lambda w, t: pltpu.VMEM((2, tm, w), t)
    outs = pl.pallas_call(
        functools.partial(_front_kernel, n_cast=len(to_cast), tiles_per_seq=tiles_per_seq),
        grid=(n_tiles + 1,),
        in_specs=[row(d), tab] + [_whole(w) for w in weights] + cast_specs,
        out_specs=[row(w) for w, _ in out_widths]
        + [pl.BlockSpec((tm, 2 * VW), lambda i: (prev(i), 0)), state, state] + cast_specs,
        out_shape=[jax.ShapeDtypeStruct((n, w), t) for w, t in out_widths]
        + [jax.ShapeDtypeStruct((n, 2 * VW), BF16), state_shape, state_shape]
        + [jax.ShapeDtypeStruct(w.shape, BF16) for w in to_cast],
        scratch_shapes=[vm(d, 2 * d, BF16),
                        two(QK, F32), two(QK, F32), two(VW, BF16), two(QK, F32),
                        two(QK, F32), two(QK, F32), two(VW, BF16), two(VW, BF16), two(VW, BF16),
                        vm(DV, QK, F32), vm(DV, QK, F32),
                        vm(tm, QK, F32), vm(tm, VW, F32), vm(tm, QK, F32), vm(tm, QK, F32), vm(tm, VW, F32),
                        pltpu.SMEM((1,), F32)],
        compiler_params=pltpu.CompilerParams(
            dimension_semantics=("arbitrary",), vmem_limit_bytes=VMEM_LIMIT),
        name="front",
    )(x, table, *weights, *to_cast)
    return outs[:5], outs[5:]


def _sample_scan_kernel(qa_ref, ka_ref, la_ref, va_ref, qr_ref, kr_ref, vr_ref, ra_ref, gr_ref, gn_ref,
                        sg0_ref, sr0_ref, o_ref, sg_ref, sr_ref, raw_ref, *, steps):
    rows = qa_ref.shape[0]
    n_seq = rows // steps
    pair = 8 // steps
    t_qk = lax.broadcasted_iota(jnp.int32, (rows, QK), 0) % steps
    t_v = lax.broadcasted_iota(jnp.int32, (rows, VW), 0) % steps
    seg = _head_segment_sum()

    def down(x, d, t):
        return x if d == 0 else jnp.where(t >= d, pltpu.roll(x, d, 0), 0.0)

    g = la_ref[...]
    b = g
    for d in range(1, steps):
        b = b + down(g, d, t_qk)
    last = jnp.where(t_qk == steps - 1, b, 0.0)
    b_last = last
    for d in range(1, steps):
        b_last = b_last + pltpu.roll(last, rows - d, 0)

    qa = qa_ref[...]
    ka = ka_ref[...]
    va = va_ref[...].astype(F32)
    qr = qr_ref[...]
    kr = kr_ref[...]
    vr = vr_ref[...].astype(F32)
    lg_lane = _head_lane_const(RET_LOG_DECAY, QK, DK)
    lg_v = _head_lane_const(RET_LOG_DECAY, VW, DV)

    oa = jnp.zeros((rows, VW), F32)
    orr = jnp.zeros((rows, VW), F32)
    for d in range(steps):
        ok = t_qk >= d
        w = jnp.where(ok, jnp.exp(jnp.where(ok, b - down(b, d, t_qk), 0.0)), 0.0)
        pa = (qa * down(ka, d, t_qk) * w).astype(BF16)
        oa = oa + _dot(pa, seg) * down(va, d, t_v)
        pr = (qr * down(kr, d, t_qk) * jnp.where(ok, jnp.exp(lg_lane * float(d)), 0.0)).astype(BF16)
        orr = orr + _dot(pr, seg) * down(vr, d, t_v)

    qta = (qa * jnp.exp(b)).astype(BF16)
    kda = (ka * jnp.exp(b_last - b)).astype(BF16)
    e_hi, e_mid, e_lo = _split3(jnp.exp(b_last))
    zero = jnp.zeros_like(e_hi)
    e_parts = jnp.where(t_qk == 0, e_hi, jnp.where(t_qk == 1, e_mid, jnp.where(t_qk == 2, e_lo, zero)))
    qtr = qr.astype(BF16)
    q_scale = jnp.exp(lg_v * (t_v.astype(F32) + 1.0))
    kdr = (kr * jnp.exp(lg_lane * (steps - 1.0 - t_qk.astype(F32)))).astype(BF16)
    ones = jnp.ones((8, DV), BF16)
    slab_row = lax.broadcasted_iota(jnp.int32, (8, 1), 0) // steps
    va_b = va_ref[...]
    vr_b = vr_ref[...]

    for p in range(n_seq // pair):
        slab = slice(8 * p, 8 * p + 8)
        for h in range(N_HEADS):
            ks = slice(h * DK, (h + 1) * DK)
            vs = slice(h * DV, (h + 1) * DV)
            inter_a = jnp.zeros((8, DV), F32)
            inter_r = jnp.zeros((8, DV), F32)
            for j in range(pair):
                s_idx = p * pair + j
                mine = slab_row == j
                s0a = sg0_ref[s_idx, h]
                s0r = sr0_ref[s_idx, h]
                inter_a = jnp.where(mine, _dot(qta[slab, ks], s0a.astype(BF16)), inter_a)
                inter_r = jnp.where(mine, _dot(qtr[slab, ks], s0r.astype(BF16)), inter_r)
                e_col = _dot_tn(jnp.where(mine, e_parts[slab, ks], 0.0).astype(BF16), ones)
                kd = jnp.where(mine, kda[slab, ks], 0.0).astype(BF16)
                sg_ref[s_idx, h] = e_col * s0a + _dot_tn(kd, va_b[slab, vs])
                kd = jnp.where(mine, kdr[slab, ks], 0.0).astype(BF16)
                sr_ref[s_idx, h] = math.exp(RET_LOG_DECAY[h] * steps) * s0r + _dot_tn(kd, vr_b[slab, vs])
            raw_ref[slab, vs] = oa[slab, vs] + inter_a
            raw_ref[slab, VW + h * DV:VW + (h + 1) * DV] = orr[slab, vs] + q_scale[slab, vs] * inter_r

    for h in range(N_HEADS):
        vs = slice(h * DV, (h + 1) * DV)
        cols = slice(VW + h * DV, VW + (h + 1) * DV)
        o_ref[:, vs] = _gated_norm(raw_ref[:, vs], gn_ref[:, vs], ra_ref[:, vs])
        o_ref[:, cols] = _gated_norm(raw_ref[:, cols], gn_ref[:, cols], gr_ref[:, vs])


def _sample_scan_call(qa, ka, la, va, qr, kr, vr, ra, gr, gn, sg0, sr0, steps):
    n_seq = sg0.shape[0]
    rows = SAMPLE_SEQS * steps
    row = lambda w: pl.BlockSpec((rows, w), lambda i: (i, 0))
    state = pl.BlockSpec((SAMPLE_SEQS, N_HEADS, DK, DV), lambda i: (i, 0, 0, 0))
    state_shape = jax.ShapeDtypeStruct(sg0.shape, F32)
    return pl.pallas_call(
        functools.partial(_sample_scan_kernel, steps=steps),
        grid=(n_seq // SAMPLE_SEQS,),
        in_specs=[row(QK), row(QK), row(QK), row(VW), row(QK), row(QK), row(VW), row(VW), row(VW), _whole(gn),
                  state, state],
        out_specs=[row(2 * VW), state, state],
        out_shape=[jax.ShapeDtypeStruct((n_seq * steps, 2 * VW), BF16), state_shape, state_shape],
        scratch_shapes=[pltpu.VMEM((rows, 2 * VW), F32)],
        compiler_params=pltpu.CompilerParams(
            dimension_semantics=("arbitrary",), vmem_limit_bytes=VMEM_LIMIT),
        name="sample_scan",
    )(qa, ka, la, va, qr, kr, vr, ra, gr, gn, sg0, sr0)


def _post_chain(rows, o_ref, h1_ref, gg_ref, p_ref,
                wout_ref, n2_ref, w2i_ref, w2o_ref, npl_ref, wpg_ref, wpp_ref, nf_ref, y_ref, final):
    d = h1_ref.shape[1]
    mix = (gg_ref[rows, :d].astype(F32) * _dot(o_ref[rows, :VW], wout_ref[:VW, :])
           + gg_ref[rows, d:].astype(F32) * _dot(o_ref[rows, VW:], wout_ref[VW:, :]))
    yield
    h = h1_ref[rows, :] + mix
    u = _rms(h, n2_ref[...]).astype(BF16)
    yield
    f = _ffn(u, w2i_ref, w2o_ref)
    yield
    h = h + 0.5 * f
    u = _rms(h, npl_ref[...]).astype(BF16)
    yield
    gate = _dot(u, wpg_ref[...])
    proj = _dot(p_ref[rows, :].astype(BF16), wpp_ref[...])
    yield
    h = h + proj * _sigmoid(gate)
    y_ref[rows, :] = _rms(h, nf_ref[...]) if final else h
    yield


_POST_ORDER = (0, 0, 1, 0, 1, 1, 0, 0, 1, 1, 0, 1)


def _post_kernel(*refs, final):
    tile = refs[0].shape[0]
    chains = [_post_chain(slice(lo, lo + tile // 2), *refs, final) for lo in (0, tile // 2)]
    for c in _POST_ORDER:
        next(chains[c])


def _post_call(o, h1, gg, p, weights, final):
    n, d = h1.shape
    tm = POST_TILE
    row = lambda w: pl.BlockSpec((tm, w), lambda i: (i, 0))
    return pl.pallas_call(
        functools.partial(_post_kernel, final=final),
        grid=(n // tm,),
        in_specs=[row(2 * VW), row(d), row(2 * d), row(p.shape[1])] + [_whole(w) for w in weights],
        out_specs=row(d),
        out_shape=jax.ShapeDtypeStruct((n, d), F32),
        compiler_params=pltpu.CompilerParams(
            dimension_semantics=("arbitrary",), vmem_limit_bytes=VMEM_LIMIT),
        name="post",
    )(o, h1, gg, p, *weights)


def _rotary_table(pos, rows):
    half = DK // 2
    freq = ROPE_BASE ** (-np.arange(half, dtype=np.float64) / half)
    ang = np.asarray(pos, np.float64)[:, None] * freq[None, :]
    cos, sin = np.cos(ang), np.sin(ang)
    reps = (rows // len(pos), N_HEADS)
    return jnp.asarray(np.concatenate([np.tile(np.concatenate([cos, cos], axis=-1), reps),
                                       np.tile(np.concatenate([-sin, sin], axis=-1), reps)], axis=-1), F32)


def kernel(x_prompt, x_sample, state_gla, state_ret, p_prompt, p_sample, norm_ffn1, w_ffn1_in, w_ffn1_out, norm_mix, w_in, w_alpha_up, b_alpha, gn_gla, gn_ret, w_out, norm_ffn2, w_ffn2_in, w_ffn2_out, norm_ple, w_ple_gate, w_ple_proj, norm_final):
    bp, tp, d = x_prompt.shape
    bs, ts, _ = x_sample.shape
    depth = w_in.shape[0]
    assert w_in.shape[2] == N_MAIN + LOW_RANK + 2 * d
    assert tp % PRE_TILE == 0 and tp % POST_TILE == 0 and tp % REF_CHUNK == 0 and 3 <= ts < REF_CHUNK and 8 % ts == 0
    assert (bs * ts) % POST_TILE == 0 and (bs * ts) % PRE_TILE == 0 and bs % SAMPLE_SEQS == 0

    table_p = _rotary_table(np.arange(tp), tp)
    table_s = _rotary_table(PAST_LEN + np.arange(ts), PRE_TILE)

    hp = x_prompt.reshape(bp * tp, d)
    hs = x_sample.reshape(bs * ts, d)
    row = lambda v: v.reshape(1, -1).astype(F32)
    gla_p, ret_p, gla_s, ret_s = [], [], [], []
    for i in range(depth):
        pre_w = (row(norm_ffn1[i]), w_ffn1_in[i].astype(BF16), w_ffn1_out[i].astype(BF16), row(norm_mix[i]),
                 w_in[i].astype(BF16), w_alpha_up[i].astype(BF16), row(b_alpha[i]))
        post_cast = (w_out[i], w_ffn2_in[i], w_ffn2_out[i], w_ple_gate[i], w_ple_proj[i])
        final = i == depth - 1

        gn = row(jnp.concatenate([gn_gla[i], gn_ret[i]]))
        (h1, gg, o, sg, sr), post_bf = _front_call(hp, table_p, (*pre_w, gn), post_cast, bp, tp)
        wout_b, w2i_b, w2o_b, wpg_b, wpp_b = post_bf
        post_w = (wout_b, row(norm_ffn2[i]), w2i_b, w2o_b, row(norm_ple[i]), wpg_b, wpp_b, row(norm_final))
        hp = _post_call(o, h1, gg, p_prompt[i].reshape(bp * tp, -1), post_w, final)
        gla_p.append(sg.astype(state_gla.dtype))
        ret_p.append(sr.astype(state_ret.dtype))

        h1, gg, qa, ka, va, la, qr, kr, vr, ra, gr = _pre_call(hs, table_s, pre_w)
        o, sg, sr = _sample_scan_call(qa, ka, la, va, qr, kr, vr, ra, gr, gn,
                                      state_gla[i].astype(F32), state_ret[i].astype(F32), ts)
        hs = _post_call(o, h1, gg, p_sample[i].reshape(bs * ts, -1), post_w, final)
        gla_s.append(sg.astype(state_gla.dtype))
        ret_s.append(sr.astype(state_ret.dtype))

    return (hp.reshape(bp, tp, d), hs.reshape(bs, ts, d),
            jnp.stack(gla_p), jnp.stack(ret_p), jnp.stack(gla_s), jnp.stack(ret_s))
```
